```python
import math
import jax, jax.numpy as jnp
from jax import lax
import numpy as np

D_MODEL = 1024
BATCH = 8
SEQ = 4096
DEPTH = 2

GRID_W = 64
CTX_LEN = 256
N_MIXERS = 2
RET_HEADS = 4
RET_QK_DIM = D_MODEL // RET_HEADS
RET_V_DIM = 2 * RET_QK_DIM
RET_CHUNK = 128
RET_IN_WIDTH = 2 * RET_HEADS * RET_QK_DIM + 3 * RET_HEADS * RET_V_DIM
DIFF_HEADS = 8
DIFF_HEAD_DIM = D_MODEL // DIFF_HEADS // 2
DIFF_IN_WIDTH = 3 * D_MODEL
Q_BLOCK = 128
ROPE_BASE = 10000.0
D_FF = 2816
N_EXPERTS = 8
TOP_K = 2
D_FF_EXPERT = 3584
EPS = 1e-6

kernel_name = "hybrid_retention_diffattn_moe_dit"


def _rms(x, g):
    xf = x.astype(jnp.float32)
    y = xf * lax.rsqrt(jnp.mean(xf * xf, axis=-1, keepdims=True) + EPS)
    return (y * g.astype(jnp.float32)).astype(x.dtype)


def _head_norm(u):
    return u * lax.rsqrt(jnp.mean(u * u, axis=-1, keepdims=True) + EPS)


def _modulate(h, shift, scale):
    return h * (1.0 + scale) + shift


def _axial_rope_tables(rows, head_dim):
    pos_r = jnp.repeat(jnp.arange(rows, dtype=jnp.float32), GRID_W)
    pos_c = jnp.tile(jnp.arange(GRID_W, dtype=jnp.float32), rows)
    axis_dim = head_dim // 2
    freqs = ROPE_BASE ** (-jnp.arange(0, axis_dim, 2, dtype=jnp.float32) / axis_dim)
    ang_r = pos_r[:, None] * freqs[None, :]
    ang_c = pos_c[:, None] * freqs[None, :]
    return (jnp.cos(ang_r), jnp.sin(ang_r), jnp.cos(ang_c), jnp.sin(ang_c))


def _rope_half(x, cos, sin):
    h = x.shape[-1] // 2
    x1, x2 = x[..., :h], x[..., h:]
    cos = cos.astype(x.dtype)
    sin = sin.astype(x.dtype)
    return jnp.concatenate([x1 * cos - x2 * sin, x2 * cos + x1 * sin], axis=-1)


def _apply_axial_rope(x, tabs):
    cr, sr, cc, sc = tabs
    a = x.shape[-1] // 2
    return jnp.concatenate([_rope_half(x[..., :a], cr, sr), _rope_half(x[..., a:], cc, sc)], axis=-1)


def _retention_chunks(q, k, v, log_gamma, state0):
    b, h, t, _ = q.shape
    dv = v.shape[-1]
    n = t // RET_CHUNK

    def to_chunks(z):
        return z.reshape(b, h, n, RET_CHUNK, z.shape[-1]).transpose(2, 0, 1, 3, 4)

    idx = jnp.arange(RET_CHUNK, dtype=jnp.float32)
    diff = idx[:, None] - idx[None, :]
    intra = jnp.where(diff >= 0, jnp.exp(log_gamma[:, None, None] * jnp.maximum(diff, 0.0)), 0.0)
    q_dec = jnp.exp(log_gamma[:, None] * (idx + 1.0))[:, :, None]
    k_dec = jnp.exp(log_gamma[:, None] * (RET_CHUNK - 1.0 - idx))[:, :, None]
    c_dec = jnp.exp(log_gamma * RET_CHUNK)[:, None, None]

    def step(state, chunk):
        qc, kc, vc = chunk
        scores = jnp.einsum('bhid,bhjd->bhij', qc, kc) * intra
        out = (jnp.einsum('bhij,bhjv->bhiv', scores, vc)
               + jnp.einsum('bhid,bhdv->bhiv', qc * q_dec, state))
        state = state * c_dec + jnp.einsum('bhjd,bhjv->bhdv', kc * k_dec, vc)
        return state, out

    state, out = lax.scan(step, state0, (to_chunks(q), to_chunks(k), to_chunks(v)))
    out = out.transpose(1, 2, 0, 3, 4).reshape(b, h, t, dv)
    return out, state


def _retention_mixer(h_lat, h_ctx, w_in, decay_logit, w_out, rope_tabs, need_ctx):
    hk = RET_HEADS * RET_QK_DIM
    hv = RET_HEADS * RET_V_DIM
    splits = [hk, 2 * hk, 2 * hk + hv, 2 * hk + 2 * hv]

    def project(hs):
        b, t, _ = hs.shape
        z = (hs @ w_in).astype(jnp.float32)
        q, k, v, gf, gb = jnp.split(z, splits, axis=-1)

        def heads(u):
            return u.reshape(b, t, RET_HEADS, -1).transpose(0, 2, 1, 3)
        return heads(q), heads(k) * RET_QK_DIM ** -0.5, heads(v), heads(gf), heads(gb)

    q_l, k_l, v_l, gf_l, gb_l = project(h_lat)
    q_c, k_c, v_c, gf_c, gb_c = project(h_ctx)
    q_l = _apply_axial_rope(q_l, rope_tabs)
    k_l = _apply_axial_rope(k_l, rope_tabs)
    log_g = jax.nn.log_sigmoid(decay_logit.astype(jnp.float32))
    zero = jnp.zeros((h_lat.shape[0], RET_HEADS, RET_QK_DIM, RET_V_DIM), jnp.float32)

    def rev(u):
        return jnp.flip(u, axis=2)

    o_cf, s_f = _retention_chunks(q_c, k_c, v_c, log_g[0], zero)
    o_lf, _ = _retention_chunks(q_l, k_l, v_l, log_g[0], s_f)
    o_cb, s_b = _retention_chunks(rev(q_c), rev(k_c), rev(v_c), log_g[1], zero)
    o_lb, _ = _retention_chunks(rev(q_l), rev(k_l), rev(v_l), log_g[1], s_b)

    def merge(of, ob, gf, gb, dtype):
        y = jax.nn.silu(gf) * _head_norm(of) + jax.nn.silu(gb) * _head_norm(rev(ob))
        bb, _, t, _ = y.shape
        return y.transpose(0, 2, 1, 3).reshape(bb, t, hv).astype(dtype) @ w_out

    y_lat = merge(o_lf, o_lb, gf_l, gb_l, h_lat.dtype)
    y_ctx = merge(o_cf, o_cb, gf_c, gb_c, h_ctx.dtype) if need_ctx else None
    return y_lat, y_ctx


def _diff_attention_block(q, k, v, lam):
    s = jnp.einsum('bhqd,bhkd->bhqk', q, k).astype(jnp.float32) * DIFF_HEAD_DIM ** -0.5
    p = jax.nn.softmax(s, axis=-1)
    b, _, nq, nk = p.shape
    p = p.reshape(b, DIFF_HEADS, 2, nq, nk)
    a = p[:, :, 0] - lam * p[:, :, 1]
    return jnp.einsum('bhqk,bhkv->bhqv', a.astype(v.dtype), v)


def _diff_attn_mixer(h_lat, h_ctx, w_in, lam_params, subln, w_out, lambda_init, rope_tabs, need_ctx):
    hq = 2 * DIFF_HEADS * DIFF_HEAD_DIM

    def project(hs):
        b, t, _ = hs.shape
        q, k, v = jnp.split(hs @ w_in, [hq, 2 * hq], axis=-1)
        q = q.reshape(b, t, 2 * DIFF_HEADS, DIFF_HEAD_DIM).transpose(0, 2, 1, 3)
        k = k.reshape(b, t, 2 * DIFF_HEADS, DIFF_HEAD_DIM).transpose(0, 2, 1, 3)
        v = v.reshape(b, t, DIFF_HEADS, 2 * DIFF_HEAD_DIM).transpose(0, 2, 1, 3)
        return q, k, v

    lp = lam_params.astype(jnp.float32)
    lam = jnp.exp(jnp.sum(lp[0] * lp[1])) - jnp.exp(jnp.sum(lp[2] * lp[3])) + lambda_init

    q_l, k_l, v_l = project(h_lat)
    q_c, k_c, v_c = project(h_ctx)
    q_l = _apply_axial_rope(q_l, rope_tabs)
    k_l = _apply_axial_rope(k_l, rope_tabs)

    k_all = jnp.concatenate([k_l, k_c], axis=2)
    v_all = jnp.concatenate([v_l, v_c], axis=2)
    b, _, t, d = q_l.shape
    n = t // Q_BLOCK
    q_blocks = q_l.reshape(b, 2 * DIFF_HEADS, n, Q_BLOCK, d).transpose(2, 0, 1, 3, 4)
    o_blocks = lax.map(lambda qb: _diff_attention_block(qb, k_all, v_all, lam), q_blocks)
    o_l = o_blocks.transpose(1, 2, 0, 3, 4).reshape(b, DIFF_HEADS, t, 2 * DIFF_HEAD_DIM)

    def finish(o, dtype):
        o = _rms(o, subln) * (1.0 - lambda_init)
        bb, _, tt, _ = o.shape
        return o.transpose(0, 2, 1, 3).reshape(bb, tt, D_MODEL).astype(dtype) @ w_out

    y_lat = finish(o_l, h_lat.dtype)
    y_ctx = finish(_diff_attention_block(q_c, k_c, v_c, lam), h_ctx.dtype) if need_ctx else None
    return y_lat, y_ctx


def _swiglu(h, w_gu, w_down):
    g, u = jnp.split(h @ w_gu, 2, axis=-1)
    return (jax.nn.silu(g) * u) @ w_down


def _moe(h, router_w, router_b, w_gu, w_down):
    logits = (h @ router_w).astype(jnp.float32) + router_b.astype(jnp.float32)
    top_v, top_i = lax.top_k(logits, TOP_K)
    top_w = jax.nn.softmax(top_v, axis=-1)
    combine = jnp.sum(jax.nn.one_hot(top_i, N_EXPERTS, dtype=jnp.float32) * top_w[..., None], axis=-2)
    out = jnp.zeros(h.shape, jnp.float32)
    for e in range(N_EXPERTS):
        out = out + combine[..., e:e + 1] * _swiglu(h, w_gu[e], w_down[e]).astype(jnp.float32)
    return out.astype(h.dtype)


def setup_inputs(seed: int = 0) -> dict:
    key = jax.random.key(seed)
    ks = jax.random.split(key, 40)

    def dense(k, shape, fan_in):
        return jax.random.normal(k, shape, jnp.float32) * fan_in ** -0.5

    def gain(k, n):
        return 1.0 + 0.05 * jax.random.normal(k, (n,), jnp.float32)

    gamma0 = 1.0 - 2.0 ** (-5.0 - np.arange(RET_HEADS))
    logit0 = jnp.asarray(np.log(gamma0 / (1.0 - gamma0)), jnp.float32)
    decay_logit = logit0[None, :] + 0.1 * jax.random.normal(ks[8], (2, RET_HEADS), jnp.float32)

    return {
        'x': jax.random.normal(ks[0], (BATCH, SEQ, D_MODEL), jnp.float32),
        'c': jax.random.normal(ks[1], (BATCH, D_MODEL), jnp.float32),
        'ctx': jax.random.normal(ks[2], (BATCH, CTX_LEN, D_MODEL), jnp.float32),
        'c_ctx': jax.random.normal(ks[3], (D_MODEL,), jnp.float32),
        'l0_w_mod': dense(ks[4], (D_MODEL, 6 * D_MODEL), D_MODEL),
        'l0_b_mod': 0.01 * jax.random.normal(ks[5], (6 * D_MODEL,), jnp.float32),
        'l0_norm_pre_mix': gain(ks[6], D_MODEL),
        'l0_norm_post_mix': gain(ks[7], D_MODEL),
        'l0_norm_pre_ffn': gain(ks[9], D_MODEL),
        'l0_norm_post_ffn': gain(ks[10], D_MODEL),
        'l0_ret_w_in': dense(ks[11], (D_MODEL, RET_IN_WIDTH), D_MODEL),
        'l0_ret_decay_logit': decay_logit,
        'l0_ret_w_out': dense(ks[12], (RET_HEADS * RET_V_DIM, D_MODEL), RET_HEADS * RET_V_DIM),
        'l0_ffn_w_gate_up': dense(ks[13], (D_MODEL, 2 * D_FF), D_MODEL),
        'l0_ffn_w_down': dense(ks[14], (D_FF, D_MODEL), D_FF),
        'l1_w_mod': dense(ks[15], (D_MODEL, 6 * D_MODEL), D_MODEL),
        'l1_b_mod': 0.01 * jax.random.normal(ks[16], (6 * D_MODEL,), jnp.float32),
        'l1_norm_pre_mix': gain(ks[17], D_MODEL),
        'l1_norm_post_mix': gain(ks[18], D_MODEL),
        'l1_norm_pre_ffn': gain(ks[19], D_MODEL),
        'l1_norm_post_ffn': gain(ks[20], D_MODEL),
        'l1_attn_w_in': dense(ks[21], (D_MODEL, DIFF_IN_WIDTH), D_MODEL),
        'l1_attn_lambda': 0.1 * jax.random.normal(ks[22], (4, DIFF_HEAD_DIM), jnp.float32),
        'l1_attn_subln': gain(ks[23], 2 * DIFF_HEAD_DIM),
        'l1_attn_w_out': dense(ks[24], (D_MODEL, D_MODEL), D_MODEL),
        'l1_router_w': dense(ks[25], (D_MODEL, N_EXPERTS), D_MODEL),
        'l1_router_b': 0.01 * jax.random.normal(ks[26], (N_EXPERTS,), jnp.float32),
        'l1_moe_w_gate_up': dense(ks[27], (N_EXPERTS, D_MODEL, 2 * D_FF_EXPERT), D_MODEL),
        'l1_moe_w_down': dense(ks[28], (N_EXPERTS, D_FF_EXPERT, D_MODEL), D_FF_EXPERT),
    }


def reference(x, c, ctx, c_ctx,
              l0_w_mod, l0_b_mod, l0_norm_pre_mix, l0_norm_post_mix, l0_norm_pre_ffn, l0_norm_post_ffn,
              l0_ret_w_in, l0_ret_decay_logit, l0_ret_w_out, l0_ffn_w_gate_up, l0_ffn_w_down,
              l1_w_mod, l1_b_mod, l1_norm_pre_mix, l1_norm_post_mix, l1_norm_pre_ffn, l1_norm_post_ffn,
              l1_attn_w_in, l1_attn_lambda, l1_attn_subln, l1_attn_w_out,
              l1_router_w, l1_router_b, l1_moe_w_gate_up, l1_moe_w_down):
    rows = x.shape[1] // GRID_W
    tabs_ret = _axial_rope_tables(rows, RET_QK_DIM)
    tabs_diff = _axial_rope_tables(rows, DIFF_HEAD_DIM)

    layers = (
        dict(w_mod=l0_w_mod, b_mod=l0_b_mod,
             norms=(l0_norm_pre_mix, l0_norm_post_mix, l0_norm_pre_ffn, l0_norm_post_ffn),
             mixer=(l0_ret_w_in, l0_ret_decay_logit, l0_ret_w_out),
             ffn=(l0_ffn_w_gate_up, l0_ffn_w_down)),
        dict(w_mod=l1_w_mod, b_mod=l1_b_mod,
             norms=(l1_norm_pre_mix, l1_norm_post_mix, l1_norm_pre_ffn, l1_norm_post_ffn),
             mixer=(l1_attn_w_in, l1_attn_lambda, l1_attn_subln, l1_attn_w_out),
             ffn=(l1_router_w, l1_router_b, l1_moe_w_gate_up, l1_moe_w_down)),
    )

    for i in range(DEPTH):
        p = layers[i]
        need_ctx = i < DEPTH - 1
        n_pre_mix, n_post_mix, n_pre_ffn, n_post_ffn = p['norms']
        m_lat = (jax.nn.silu(c) @ p['w_mod'] + p['b_mod'])[:, None, :]
        m_ctx = (jax.nn.silu(c_ctx) @ p['w_mod'] + p['b_mod'])[None, None, :]
        sh1, sc1, g1, sh2, sc2, g2 = jnp.split(m_lat, 6, axis=-1)
        sh1c, sc1c, g1c, sh2c, sc2c, g2c = jnp.split(m_ctx, 6, axis=-1)

        h_l = _modulate(_rms(x, n_pre_mix), sh1, sc1)
        h_c = _modulate(_rms(ctx, n_pre_mix), sh1c, sc1c)
        if i % N_MIXERS == 0:
            w_in, decay_logit, w_out = p['mixer']
            y_l, y_c = _retention_mixer(h_l, h_c, w_in, decay_logit, w_out, tabs_ret, need_ctx)
        else:
            w_in, lam_params, subln, w_out = p['mixer']
            lambda_init = 0.8 - 0.6 * math.exp(-0.3 * i)
            y_l, y_c = _diff_attn_mixer(h_l, h_c, w_in, lam_params, subln, w_out, lambda_init, tabs_diff, need_ctx)
        x = x + g1 * _rms(y_l, n_post_mix)
        if need_ctx:
            ctx = ctx + g1c * _rms(y_c, n_post_mix)

        h_l = _modulate(_rms(x, n_pre_ffn), sh2, sc2)
        if i % 2 == 0:
            ffn = lambda h: _swiglu(h, *p['ffn'])
        else:
            ffn = lambda h: _moe(h, *p['ffn'])
        x = x + g2 * _rms(ffn(h_l), n_post_ffn)
        if need_ctx:
            h_c = _modulate(_rms(ctx, n_pre_ffn), sh2c, sc2c)
            ctx = ctx + g2c * _rms(ffn(h_c), n_post_ffn)
    return x
```

```python
import functools
import math

import jax
import jax.numpy as jnp
from jax import lax
from jax.experimental import pallas as pl
from jax.experimental.pallas import tpu as pltpu

F32 = jnp.float32
BF16 = jnp.bfloat16
U32 = jnp.uint32
I32 = jnp.int32

EPS = 1e-6
GRID_W = 64
ROPE_BASE = 10000.0
RET_HEADS = 4
DIFF_HEADS = 8
DIFF_HEAD_DIM = 64
N_EXPERTS = 8
LANES = 128
NEG_BIG = -1e30
VMEM_LIMIT = 56 * 1024 * 1024

SH1, SC1, G1, SH2, SC2, G2 = range(6)


def _cparams(sem):
    return pltpu.CompilerParams(dimension_semantics=sem, vmem_limit_bytes=VMEM_LIMIT)


def _rms_rows(x, gain):
    return x * lax.rsqrt(jnp.mean(x * x, axis=-1, keepdims=True) + EPS) * gain


def _silu(x):
    return x * jax.nn.sigmoid(x)


def _mod_kernel(c_ref, w_ref, b_ref, o_ref):
    a = _silu(c_ref[...])
    o_ref[...] = jnp.dot(a, w_ref[...], preferred_element_type=F32,
                         precision=lax.Precision.HIGHEST) + b_ref[...]


def _modulation(cc, w_mod, b_mod):
    r, d = cc.shape
    n = w_mod.shape[1]
    tn = n // 6
    out = pl.pallas_call(
        _mod_kernel,
        grid=(n // tn,),
        in_specs=[pl.BlockSpec((r, d), lambda j: (0, 0)),
                  pl.BlockSpec((d, tn), lambda j: (0, j)),
                  pl.BlockSpec((1, tn), lambda j: (0, j))],
        out_specs=pl.BlockSpec((r, tn), lambda j: (0, j)),
        out_shape=jax.ShapeDtypeStruct((r, n), F32),
        compiler_params=_cparams(("arbitrary",)),
        name="mod",
    )(cc, w_mod, b_mod.reshape(1, n))
    return out.reshape(r, 6, d)


def _inproj_kernel(x_ref, g_ref, mod_ref, w_ref, tab_ref, *rest, rope_scales, shifts, period, aliased):
    o_ref, h_scr = rest[-2], rest[-1]
    j = pl.program_id(2)

    @pl.when(j == 0)
    def _():
        y = _rms_rows(x_ref[0], g_ref[...])
        h = y * (1.0 + mod_ref[0, SC1:SC1 + 1, :]) + mod_ref[0, SH1:SH1 + 1, :]
        h_scr[...] = h.astype(BF16)

    acc = jnp.dot(h_scr[...], w_ref[...], preferred_element_type=F32)
    tn = acc.shape[1]

    def rope_store(scale):
        for k in range(tn // LANES):
            xs = acc[:, k * LANES:(k + 1) * LANES]
            p = (k * LANES) % period
            r = xs * tab_ref[0, :, p:p + LANES]
            for si, sft in enumerate(shifts):
                r = r + pltpu.roll(xs, sft, 1) * tab_ref[1 + si, :, p:p + LANES]
            if scale != 1.0:
                r = r * scale
            o_ref[0, :, k * LANES:(k + 1) * LANES] = r.astype(o_ref.dtype)

    for jj, scale in enumerate(rope_scales):
        pl.when(j == jj)(functools.partial(rope_store, scale))

    @pl.when(j >= len(rope_scales))
    def _():
        o_ref[0] = acc.astype(o_ref.dtype)


def _inproj(x, gain, mod, mod_row, w, tab, tab_per_tile, out_tokens, out_off, tm, tn,
            rope_scales, shifts, prev=None):
    b, t, d = x.shape
    n = w.shape[1]
    period = tab.shape[2]
    kern = functools.partial(_inproj_kernel, rope_scales=rope_scales, shifts=shifts, period=period,
                             aliased=prev is not None)
    tab_map = (lambda bb, i, j: (0, i, 0)) if tab_per_tile else (lambda bb, i, j: (0, 0, 0))
    in_specs = [pl.BlockSpec((1, tm, d), lambda bb, i, j: (bb, i, 0)),
                pl.BlockSpec((1, d), lambda bb, i, j: (0, 0)),
                pl.BlockSpec((1, 6, d), lambda bb, i, j: (mod_row(bb), 0, 0)),
                pl.BlockSpec((d, tn), lambda bb, i, j: (0, j)),
                pl.BlockSpec((tab.shape[0], tm, period), tab_map)]
    args = [x, gain.reshape(1, d), mod, w, tab]
    aliases = {}
    if prev is not None:
        in_specs.append(pl.BlockSpec(memory_space=pl.ANY))
        args.append(prev)
        aliases = {5: 0}
    return pl.pallas_call(
        kern,
        grid=(b, t // tm, n // tn),
        in_specs=in_specs,
        out_specs=pl.BlockSpec((1, tm, tn), lambda bb, i, j: (bb, i + out_off, j)),
        out_shape=jax.ShapeDtypeStruct((b, out_tokens, n), BF16),
        scratch_shapes=[pltpu.VMEM((tm, d), BF16)],
        input_output_aliases=aliases,
        compiler_params=_cparams(("arbitrary", "arbitrary", "arbitrary")),
        name="inproj",
    )(*args)


def _ret_kernel(qf, kf, vf, gf, qb, kb, vb, gb, intra, qdec, kdec, cdec, yf, yb, st):
    s = pl.program_id(2)

    @pl.when(s == 0)
    def _():
        st[...] = jnp.zeros(st.shape, F32)

    for d, (q_ref, k_ref, v_ref, g_ref, y_ref) in enumerate(((qf, kf, vf, gf, yf), (qb, kb, vb, gb, yb))):
        q = q_ref[0]
        k = k_ref[0]
        v = v_ref[0]
        scores = lax.dot_general(q, k, (((1,), (1,)), ((), ())), preferred_element_type=F32) * intra[d, 0]
        state = st[d]
        o = (jnp.dot(scores.astype(BF16), v, preferred_element_type=F32)
             + qdec[d, 0] * jnp.dot(q, state.astype(BF16), preferred_element_type=F32))
        ks = (k.astype(F32) * kdec[d, 0]).astype(BF16)
        st[d] = state * cdec[d, 0] + lax.dot_general(ks, v, (((0,), (0,)), ((), ())),
                                                     preferred_element_type=F32)
        hn = o * lax.rsqrt(jnp.mean(o * o, axis=-1, keepdims=True) + EPS)
        y_ref[0] = (_silu(g_ref[0].astype(F32)) * hn).astype(y_ref.dtype)


def _retention(z, decay_logit, n_lat_chunks, chunk):
    b, t_all, _ = z.shape
    h = RET_HEADS
    dk = z.shape[2] // (8 * h)
    dv = 2 * dk
    nc = n_lat_chunks + 1
    assert t_all == nc * chunk

    log_g = jax.nn.log_sigmoid(decay_logit.astype(F32))
    idx = jnp.arange(chunk, dtype=F32)
    diff = idx[:, None] - idx[None, :]
    lg = log_g[:, :, None, None]
    intra_f = jnp.where(diff >= 0, jnp.exp(lg[0] * jnp.maximum(diff, 0.0)), 0.0)
    intra_b = jnp.where(diff <= 0, jnp.exp(lg[1] * jnp.maximum(-diff, 0.0)), 0.0)
    intra = jnp.stack([intra_f, intra_b])
    qdec = jnp.stack([jnp.exp(log_g[0][:, None] * (idx + 1.0)),
                      jnp.exp(log_g[1][:, None] * (chunk - idx))])[..., None]
    kdec = jnp.stack([jnp.exp(log_g[0][:, None] * (chunk - 1.0 - idx)),
                      jnp.exp(log_g[1][:, None] * idx)])[..., None]
    cdec = jnp.exp(log_g * chunk)[..., None, None]

    def fwd(s):
        return jnp.where(s == 0, n_lat_chunks, s - 1)

    def bwd(s):
        return jnp.where(s == 0, n_lat_chunks, n_lat_chunks - s)

    def spec(width, col0, chunk_of):
        return pl.BlockSpec((1, chunk, width), lambda bb, hh, s: (bb, chunk_of(s), col0 + hh))

    in_specs = [spec(dk, 0, fwd), spec(dk, h, fwd), spec(dv, h, fwd), spec(dv, 2 * h, fwd),
                spec(dk, 0, bwd), spec(dk, h, bwd), spec(dv, h, bwd), spec(dv, 3 * h, bwd),
                pl.BlockSpec((2, 1, chunk, chunk), lambda bb, hh, s: (0, hh, 0, 0)),
                pl.BlockSpec((2, 1, chunk, 1), lambda bb, hh, s: (0, hh, 0, 0)),
                pl.BlockSpec((2, 1, chunk, 1), lambda bb, hh, s: (0, hh, 0, 0)),
                pl.BlockSpec((2, 1, 1, 1), lambda bb, hh, s: (0, hh, 0, 0))]
    out_specs = [pl.BlockSpec((1, chunk, dv), lambda bb, hh, s: (bb, fwd(s), hh)),
                 pl.BlockSpec((1, chunk, dv), lambda bb, hh, s: (bb, bwd(s), hh))]
    yshape = jax.ShapeDtypeStruct((b, t_all, h * dv), BF16)
    return pl.pallas_call(
        _ret_kernel,
        grid=(b, h, nc),
        in_specs=in_specs,
        out_specs=out_specs,
        out_shape=[yshape, yshape],
        scratch_shapes=[pltpu.VMEM((2, dk, dv), F32)],
        compiler_params=_cparams(("arbitrary", "arbitrary", "arbitrary")),
        name="retention",
    )(z, z, z, z, z, z, z, z, intra, qdec, kdec, cdec)


def _pack_rows(h):
    n = h.shape[1] // 2
    bits = lax.bitcast_convert_type(h.astype(BF16).astype(F32), U32)
    return (bits[:, :n] >> 16) | (bits[:, n:] & jnp.uint32(0xFFFF0000))


def _unpack_rows(w):
    lo = lax.bitcast_convert_type(w << 16, F32)
    hi = lax.bitcast_convert_type(w & jnp.uint32(0xFFFF0000), F32)
    return lo, hi


def _outproj_kernel(*refs, n_y, route):
    y_refs = refs[:n_y]
    w_ref, x_ref, gpost_ref, gpre_ref, mod_ref = refs[n_y:n_y + 5]
    rest = refs[n_y + 5:]
    if route:
        rw_ref, rb_ref, xo_ref, hp_ref, rt_ref = rest
    else:
        xo_ref, h_ref = rest
    y = y_refs[0][0]
    if n_y == 2:
        y = (y.astype(F32) + y_refs[1][0].astype(F32)).astype(BF16)
    o = jnp.dot(y, w_ref[...], preferred_element_type=F32)
    xn = x_ref[0] + mod_ref[0, G1:G1 + 1, :] * _rms_rows(o, gpost_ref[...])
    xo_ref[0] = xn
    h = _rms_rows(xn, gpre_ref[...]) * (1.0 + mod_ref[0, SC2:SC2 + 1, :]) + mod_ref[0, SH2:SH2 + 1, :]
    if not route:
        h_ref[0] = h.astype(h_ref.dtype)
        return
    hp_ref[0] = _pack_rows(h)
    logits = jnp.dot(h, rw_ref[...], preferred_element_type=F32,
                     precision=lax.Precision.HIGHEST) + rb_ref[...]
    lane = lax.broadcasted_iota(I32, logits.shape, 1).astype(F32)
    m1 = jnp.max(logits, axis=-1, keepdims=True)
    i1 = jnp.min(jnp.where(logits == m1, lane, float(LANES)), axis=-1, keepdims=True)
    masked = jnp.where(lane == i1, NEG_BIG, logits)
    m2 = jnp.max(masked, axis=-1, keepdims=True)
    i2 = jnp.min(jnp.where(masked == m2, lane, float(LANES)), axis=-1, keepdims=True)
    e2 = jnp.exp(m2 - m1)
    den = 1.0 + e2
    w1 = 1.0 / den
    w2 = e2 / den
    rt_ref[0] = jnp.where(lane == 0.0, i1,
                          jnp.where(lane == 1.0, i2,
                                    jnp.where(lane == 2.0, w1, jnp.where(lane == 3.0, w2, 0.0))))


def _outproj(ys, y_off, w, x, gpost, gpre, mod, mod_row, tm, router=None):
    b, t, d = x.shape
    kdim = w.shape[0]
    route = router is not None
    in_specs = [pl.BlockSpec((1, tm, kdim), lambda bb, i: (bb, i + y_off, 0)) for _ in ys]
    in_specs += [pl.BlockSpec((kdim, d), lambda bb, i: (0, 0)),
                 pl.BlockSpec((1, tm, d), lambda bb, i: (bb, i, 0)),
                 pl.BlockSpec((1, d), lambda bb, i: (0, 0)),
                 pl.BlockSpec((1, d), lambda bb, i: (0, 0)),
                 pl.BlockSpec((1, 6, d), lambda bb, i: (mod_row(bb), 0, 0))]
    args = list(ys) + [w, x, gpost.reshape(1, d), gpre.reshape(1, d), mod]
    out_specs = [pl.BlockSpec((1, tm, d), lambda bb, i: (bb, i, 0))]
    out_shape = [jax.ShapeDtypeStruct((b, t, d), F32)]
    if route:
        rw, rb = router
        in_specs += [pl.BlockSpec((d, LANES), lambda bb, i: (0, 0)),
                     pl.BlockSpec((1, LANES), lambda bb, i: (0, 0))]
        args += [rw, rb]
        out_specs += [pl.BlockSpec((1, tm, d // 2), lambda bb, i: (bb, i, 0)),
                      pl.BlockSpec((1, tm, LANES), lambda bb, i: (bb, i, 0))]
        out_shape += [jax.ShapeDtypeStruct((b, t, d // 2), U32),
                      jax.ShapeDtypeStruct((b, t, LANES), F32)]
    else:
        out_specs.append(pl.BlockSpec((1, tm, d), lambda bb, i: (bb, i, 0)))
        out_shape.append(jax.ShapeDtypeStruct((b, t, d), BF16))
    return pl.pallas_call(
        functools.partial(_outproj_kernel, n_y=len(ys), route=route),
        grid=(b, t // tm),
        in_specs=in_specs,
        out_specs=out_specs,
        out_shape=out_shape,
        compiler_params=_cparams(("arbitrary", "arbitrary")),
        name="outproj",
    )(*args)


def _ffn_kernel(h_ref, wg_ref, wu_ref, wd_ref, x_ref, gpost_ref, mod_ref, o_ref, acc):
    f = pl.program_id(2)
    h = h_ref[0]
    g = jnp.dot(h, wg_ref[...], preferred_element_type=F32)
    u = jnp.dot(h, wu_ref[...], preferred_element_type=F32)
    a = (_silu(g) * u).astype(BF16)
    part = jnp.dot(a, wd_ref[...], preferred_element_type=F32)

    @pl.when(f == 0)
    def _():
        acc[...] = part

    @pl.when(f > 0)
    def _():
        acc[...] += part

    @pl.when(f == pl.num_programs(2) - 1)
    def _():
        o_ref[0] = x_ref[0] + mod_ref[0, G2:G2 + 1, :] * _rms_rows(acc[...], gpost_ref[...])


def _ffn(h, w_gu, w_down, x, gpost, mod, mod_row, tm, nf):
    b, t, d = x.shape
    dff = w_down.shape[0]
    tf = dff // nf
    return pl.pallas_call(
        _ffn_kernel,
        grid=(b, t // tm, nf),
        in_specs=[pl.BlockSpec((1, tm, d), lambda bb, i, f: (bb, i, 0)),
                  pl.BlockSpec((d, tf), lambda bb, i, f: (0, f)),
                  pl.BlockSpec((d, tf), lambda bb, i, f: (0, f + nf)),
                  pl.BlockSpec((tf, d), lambda bb, i, f: (f, 0)),
                  pl.BlockSpec((1, tm, d), lambda bb, i, f: (bb, i, 0)),
                  pl.BlockSpec((1, d), lambda bb, i, f: (0, 0)),
                  pl.BlockSpec((1, 6, d), lambda bb, i, f: (mod_row(bb), 0, 0))],
        out_specs=pl.BlockSpec((1, tm, d), lambda bb, i, f: (bb, i, 0)),
        out_shape=jax.ShapeDtypeStruct((b, t, d), F32),
        scratch_shapes=[pltpu.VMEM((tm, d), F32)],
        compiler_params=_cparams(("arbitrary", "arbitrary", "arbitrary")),
        name="ffn",
    )(h, w_gu, w_gu, w_down, x, gpost.reshape(1, d), mod)


def _attn_kernel(q_ref, k_ref, v_ref, lam_ref, subln_ref, o_ref, *, lambda_init):
    q = q_ref[0]
    k = k_ref[0]
    v = v_ref[0]
    lane = lax.broadcasted_iota(I32, q.shape, 1)
    zero = jnp.zeros_like(q)
    lp = lam_ref[...]
    lam = (jnp.exp(jnp.sum(lp[0:1] * lp[1:2], axis=-1, keepdims=True))
           - jnp.exp(jnp.sum(lp[2:3] * lp[3:4], axis=-1, keepdims=True)) + lambda_init)

    def one(qh):
        s = lax.dot_general(qh, k, (((1,), (1,)), ((), ())), preferred_element_type=F32)
        m = jnp.max(s, axis=-1, keepdims=True)
        p = jnp.exp(s - m)
        l = jnp.sum(p, axis=-1, keepdims=True)
        return jnp.dot(p.astype(BF16), v, preferred_element_type=F32) / l

    o = one(jnp.where(lane < DIFF_HEAD_DIM, q, zero)) - lam * one(jnp.where(lane >= DIFF_HEAD_DIM, q, zero))
    o_ref[0] = (_rms_rows(o, subln_ref[...]) * (1.0 - lambda_init)).astype(o_ref.dtype)


def _attention(z, lam_params, subln, n_lat, tq, lambda_init):
    b, t_all, n3 = z.shape
    d = n3 // 3
    npair = d // LANES
    return pl.pallas_call(
        functools.partial(_attn_kernel, lambda_init=lambda_init),
        grid=(b, npair, n_lat // tq),
        in_specs=[pl.BlockSpec((1, tq, LANES), lambda bb, p, i: (bb, i, p)),
                  pl.BlockSpec((1, t_all, LANES), lambda bb, p, i: (bb, 0, npair + p)),
                  pl.BlockSpec((1, t_all, LANES), lambda bb, p, i: (bb, 0, 2 * npair + p)),
                  pl.BlockSpec(lam_params.shape, lambda bb, p, i: (0, 0)),
                  pl.BlockSpec((1, LANES), lambda bb, p, i: (0, 0))],
        out_specs=pl.BlockSpec((1, tq, LANES), lambda bb, p, i: (bb, i, p)),
        out_shape=jax.ShapeDtypeStruct((b, n_lat, d), BF16),
        compiler_params=_cparams(("arbitrary", "arbitrary", "arbitrary")),
        name="attention",
    )(z, z, z, lam_params.astype(F32), subln.reshape(1, LANES).astype(F32))


def _moe_kernel(te_ref, tv_ref, src_ref, nsrc_ref, dst_ref, wrow_ref, wg_ref, wu_ref, wd_ref, h_hbm,
                y_hbm, gbuf, hbuf, acc, obuf, gsem, ssem, *, tm, nt):
    del te_ref
    j = pl.program_id(0)
    f = pl.program_id(1)
    nf = pl.num_programs(1)
    slot = j % 2
    valid = tv_ref[j] > 0

    def start_gather(idx_ref, to_slot):
        def body(r, c):
            pltpu.make_async_copy(h_hbm.at[pl.ds(idx_ref[0, 0, r], 1)],
                                  gbuf.at[to_slot, pl.ds(r, 1)], gsem.at[to_slot]).start()
            return c
        lax.fori_loop(0, tm, body, 0)

    def wait_scatter():
        pltpu.make_async_copy(obuf, y_hbm.at[pl.ds(0, tm)], ssem.at[0]).wait()

    @pl.when(jnp.logical_and(valid, f == 0))
    def _():
        @pl.when(j == 0)
        def _():
            start_gather(src_ref, 0)

        pltpu.make_async_copy(h_hbm.at[pl.ds(0, tm)], gbuf.at[slot], gsem.at[slot]).wait()
        lo, hi = _unpack_rows(gbuf[slot])
        half = lo.shape[1]
        hbuf[:, :half] = lo.astype(BF16)
        hbuf[:, half:] = hi.astype(BF16)

        @pl.when(tv_ref[jnp.minimum(j + 1, nt - 1)] * (j + 1 < nt).astype(I32) > 0)
        def _():
            start_gather(nsrc_ref, 1 - slot)

    @pl.when(valid)
    def _():
        h = hbuf[...]
        g = jnp.dot(h, wg_ref[0], preferred_element_type=F32)
        u = jnp.dot(h, wu_ref[0], preferred_element_type=F32)
        a = (_silu(g) * u).astype(BF16)
        part = jnp.dot(a, wd_ref[0], preferred_element_type=F32)

        @pl.when(f == 0)
        def _():
            acc[...] = part

        @pl.when(f > 0)
        def _():
            acc[...] += part

    @pl.when(jnp.logical_and(valid, f == nf - 1))
    def _():
        @pl.when(j > 0)
        def _():
            wait_scatter()

        obuf[...] = _pack_rows(acc[...] * wrow_ref[...])

        def body(r, c):
            pltpu.make_async_copy(obuf.at[pl.ds(r, 1)], y_hbm.at[pl.ds(dst_ref[0, 0, r], 1)],
                                  ssem.at[0]).start()
            return c
        lax.fori_loop(0, tm, body, 0)

        last = tv_ref[jnp.minimum(j + 1, nt - 1)] * (j + 1 < nt).astype(I32) == 0

        @pl.when(last)
        def _():
            wait_scatter()


def _moe(hp, te, tv, src, dst, wrow, w_gu, w_down, n_out_rows, tm, tf):
    nt = te.shape[0]
    e, d, two_ff = w_gu.shape
    dff = two_ff // 2
    nf = dff // tf
    half = hp.shape[1]

    def wf(j, f, te_r, tv_r):
        return jnp.where(tv_r[j] > 0, f, nf - 1)

    grid_spec = pltpu.PrefetchScalarGridSpec(
        num_scalar_prefetch=2,
        grid=(nt, nf),
        in_specs=[pl.BlockSpec((1, 1, tm), lambda j, f, a, c: (j, 0, 0), memory_space=pltpu.SMEM),
                  pl.BlockSpec((1, 1, tm), lambda j, f, a, c: (jnp.minimum(j + 1, nt - 1), 0, 0),
                               memory_space=pltpu.SMEM),
                  pl.BlockSpec((1, 1, tm), lambda j, f, a, c: (j, 0, 0), memory_space=pltpu.SMEM),
                  pl.BlockSpec((tm, 1), lambda j, f, a, c: (j, 0)),
                  pl.BlockSpec((1, d, tf), lambda j, f, a, c: (a[j], 0, wf(j, f, a, c))),
                  pl.BlockSpec((1, d, tf), lambda j, f, a, c: (a[j], 0, wf(j, f, a, c) + nf)),
                  pl.BlockSpec((1, tf, d), lambda j, f, a, c: (a[j], wf(j, f, a, c), 0)),
                  pl.BlockSpec(memory_space=pl.ANY)],
        out_specs=pl.BlockSpec(memory_space=pl.ANY),
        scratch_shapes=[pltpu.VMEM((2, tm, half), U32),
                        pltpu.VMEM((tm, d), BF16),
                        pltpu.VMEM((tm, d), F32),
                        pltpu.VMEM((tm, half), U32),
                        pltpu.SemaphoreType.DMA((2,)),
                        pltpu.SemaphoreType.DMA((1,))],
    )
    return pl.pallas_call(
        functools.partial(_moe_kernel, tm=tm, nt=nt),
        grid_spec=grid_spec,
        out_shape=jax.ShapeDtypeStruct((n_out_rows, half), U32),
        compiler_params=_cparams(("arbitrary", "arbitrary")),
        name="moe",
    )(te, tv, src, src, dst, wrow, w_gu, w_gu, w_down, hp)


def _route_plan(rt, tm, nt):
    t = rt.shape[0]
    e_flat = rt[:, 0:2].astype(I32).reshape(-1)
    w_flat = rt[:, 2:4].reshape(-1)
    onehot = (e_flat[:, None] == jnp.arange(N_EXPERTS, dtype=I32)[None, :]).astype(I32)
    csum = jnp.cumsum(onehot, axis=0)
    rank = jnp.take_along_axis(csum, e_flat[:, None], axis=1)[:, 0] - 1
    counts = csum[-1]
    tiles_e = (counts + tm - 1) // tm
    tile_end = jnp.cumsum(tiles_e)
    row_off = (tile_end - tiles_e) * tm
    dest = row_off[e_flat] + rank
    a = jnp.arange(2 * t, dtype=I32)
    p = nt * tm
    src = jnp.zeros((p,), I32).at[dest].set(a // 2)
    dst = (2 * t + jnp.arange(p, dtype=I32) % tm).at[dest].set((a % 2) * t + a // 2)
    wrow = jnp.zeros((p,), F32).at[dest].set(w_flat)
    n_used = tile_end[-1]
    tid = jnp.arange(nt, dtype=I32)
    tv = (tid < n_used).astype(I32)
    te = jnp.minimum(jnp.searchsorted(tile_end, jnp.minimum(tid, n_used - 1), side="right"),
                     N_EXPERTS - 1).astype(I32)
    return te, tv, src.reshape(nt, 1, tm), dst.reshape(nt, 1, tm), wrow.reshape(p, 1)


def _combine_kernel(y0_ref, y1_ref, x_ref, gpost_ref, mod_ref, o_ref):
    lo0, hi0 = _unpack_rows(y0_ref[...])
    lo1, hi1 = _unpack_rows(y1_ref[...])
    y = jnp.concatenate([lo0 + lo1, hi0 + hi1], axis=1)
    o_ref[0] = x_ref[0] + mod_ref[0, G2:G2 + 1, :] * _rms_rows(y, gpost_ref[...])


def _combine(y2, x, gpost, mod, tm):
    b, t, d = x.shape
    per_b = t // tm
    nblk = b * per_b
    return pl.pallas_call(
        _combine_kernel,
        grid=(b, per_b),
        in_specs=[pl.BlockSpec((tm, d // 2), lambda bb, i: (bb * per_b + i, 0)),
                  pl.BlockSpec((tm, d // 2), lambda bb, i: (nblk + bb * per_b + i, 0)),
                  pl.BlockSpec((1, tm, d), lambda bb, i: (bb, i, 0)),
                  pl.BlockSpec((1, d), lambda bb, i: (0, 0)),
                  pl.BlockSpec((1, 6, d), lambda bb, i: (bb, 0, 0))],
        out_specs=pl.BlockSpec((1, tm, d), lambda bb, i: (bb, i, 0)),
        out_shape=jax.ShapeDtypeStruct((b, t, d), F32),
        compiler_params=_cparams(("arbitrary", "arbitrary")),
        name="combine",
    )(y2, y2, x, gpost.reshape(1, d), mod)


def _rope_angles(rows, head_dim):
    pos_r = jnp.repeat(jnp.arange(rows, dtype=F32), GRID_W)
    pos_c = jnp.tile(jnp.arange(GRID_W, dtype=F32), rows)
    axis_dim = head_dim // 2
    freqs = ROPE_BASE ** (-jnp.arange(0, axis_dim, 2, dtype=F32) / axis_dim)
    return pos_r[:, None] * freqs[None, :], pos_c[:, None] * freqs[None, :]


def _rope_tables(rows, head_dim, reps):
    ar, ac = _rope_angles(rows, head_dim)
    cos = jnp.concatenate([jnp.cos(ar), jnp.cos(ar), jnp.cos(ac), jnp.cos(ac)], axis=1)
    zr = jnp.zeros_like(ar)
    s_up = jnp.concatenate([-jnp.sin(ar), zr, -jnp.sin(ac), zr], axis=1)
    s_dn = jnp.concatenate([zr, jnp.sin(ar), zr, jnp.sin(ac)], axis=1)
    return jnp.stack([jnp.tile(cos, (1, reps)), jnp.tile(s_up, (1, reps)), jnp.tile(s_dn, (1, reps))])


def _identity_tables(n_tab, tm, period):
    return jnp.concatenate([jnp.ones((1, tm, period), F32), jnp.zeros((n_tab - 1, tm, period), F32)])


def _tile(t, want):
    return min(t, want)


def kernel(x, c, ctx, c_ctx, l0_w_mod, l0_b_mod, l0_norm_pre_mix, l0_norm_post_mix, l0_norm_pre_ffn, l0_norm_post_ffn, l0_ret_w_in, l0_ret_decay_logit, l0_ret_w_out, l0_ffn_w_gate_up, l0_ffn_w_down, l1_w_mod, l1_b_mod, l1_norm_pre_mix, l1_norm_post_mix, l1_norm_pre_ffn, l1_norm_post_ffn, l1_attn_w_in, l1_attn_lambda, l1_attn_subln, l1_attn_w_out, l1_router_w, l1_router_b, l1_moe_w_gate_up, l1_moe_w_down):
    b, t, d = x.shape
    tc = ctx.shape[1]
    rows = t // GRID_W
    t_all = t + tc
    assert t % tc == 0
    n_lat_chunks = t // tc

    n_rows = -(-(b + 1) // 8) * 8
    cc = jnp.concatenate([c, c_ctx[None, :], jnp.zeros((n_rows - b - 1, d), F32)], axis=0)
    lat_row = lambda bb: bb
    ctx_row = lambda bb: b

    tm_lat = _tile(t, 1024)
    tn = 1024

    mod0 = _modulation(cc, l0_w_mod, l0_b_mod)
    w_in0 = l0_ret_w_in.astype(BF16)
    dk = w_in0.shape[1] // (8 * RET_HEADS)
    tab_ret = _rope_tables(rows, dk, 1)
    tab_ret = jnp.stack([tab_ret[0], tab_ret[1] + tab_ret[2]])
    ret_scales = (1.0,) * (RET_HEADS * dk // tn) + (dk ** -0.5,) * (RET_HEADS * dk // tn)
    z0 = _inproj(x, l0_norm_pre_mix, mod0, lat_row, w_in0, tab_ret, True, t_all, 0, tm_lat, tn,
                 ret_scales, (dk // 4,))
    z0 = _inproj(ctx, l0_norm_pre_mix, mod0, ctx_row, w_in0, _identity_tables(2, tc, dk), False, t_all,
                 t // tc, tc, tn, ret_scales, (dk // 4,), prev=z0)
    yf, yb = _retention(z0, l0_ret_decay_logit, n_lat_chunks, tc)

    w_out0 = l0_ret_w_out.astype(BF16)
    tm_o = _tile(t, 512)
    x1, h1 = _outproj([yf, yb], 0, w_out0, x, l0_norm_post_mix, l0_norm_pre_ffn, mod0, lat_row, tm_o)
    c1, hc1 = _outproj([yf, yb], t // tc, w_out0, ctx, l0_norm_post_mix, l0_norm_pre_ffn, mod0, ctx_row, tc)

    w_gu0 = l0_ffn_w_gate_up.astype(BF16)
    w_dn0 = l0_ffn_w_down.astype(BF16)
    x2 = _ffn(h1, w_gu0, w_dn0, x1, l0_norm_post_ffn, mod0, lat_row, tm_lat, 2)
    c2 = _ffn(hc1, w_gu0, w_dn0, c1, l0_norm_post_ffn, mod0, ctx_row, tc, 2)

    lambda_init = 0.8 - 0.6 * math.exp(-0.3 * 1)
    mod1 = _modulation(cc, l1_w_mod, l1_b_mod)
    w_in1 = l1_attn_w_in.astype(BF16)
    tab_diff = _rope_tables(rows, DIFF_HEAD_DIM, LANES // DIFF_HEAD_DIM)
    q4 = DIFF_HEAD_DIM // 4
    diff_scales = (DIFF_HEAD_DIM ** -0.5,) * (d // tn) + (1.0,) * (d // tn)
    z1 = _inproj(x2, l1_norm_pre_mix, mod1, lat_row, w_in1, tab_diff, True, t_all, 0, tm_lat, tn,
                 diff_scales, (LANES - q4, q4))
    z1 = _inproj(c2, l1_norm_pre_mix, mod1, ctx_row, w_in1, _identity_tables(3, tc, LANES), False, t_all,
                 t // tc, tc, tn, (1.0,) * (2 * d // tn), (LANES - q4, q4), prev=z1)
    o1 = _attention(z1, l1_attn_lambda, l1_attn_subln, t, _tile(t, 256), lambda_init)

    rw = jnp.zeros((d, LANES), F32).at[:, :N_EXPERTS].set(l1_router_w)
    rb = jnp.full((1, LANES), NEG_BIG, F32).at[0, :N_EXPERTS].set(l1_router_b)
    x3, hp, rt = _outproj([o1], 0, l1_attn_w_out.astype(BF16), x2, l1_norm_post_mix, l1_norm_pre_ffn, mod1,
                          lat_row, tm_o, router=(rw, rb))

    n_tok = b * t
    tm_e = _tile(n_tok, 1024)
    nt = 2 * n_tok // tm_e + N_EXPERTS
    te, tv, src, dst, wrow = _route_plan(rt.reshape(n_tok, LANES), tm_e, nt)
    dff_e = l1_moe_w_down.shape[1]
    y2 = _moe(hp.reshape(n_tok, d // 2), te, tv, src, dst, wrow, l1_moe_w_gate_up.astype(BF16),
              l1_moe_w_down.astype(BF16), 2 * n_tok + tm_e, tm_e, dff_e // 7)
    return _combine(y2, x3, l1_norm_post_ffn, mod1, tm_o)
```

```python
import functools
import math

import jax
import jax.numpy as jnp
from jax import lax
from jax.experimental import pallas as pl
from jax.experimental.pallas import tpu as pltpu

F32 = jnp.float32
BF16 = jnp.bfloat16
U32 = jnp.uint32
I32 = jnp.int32

EPS = 1e-6
GRID_W = 64
ROPE_BASE = 10000.0
RET_HEADS = 4
DIFF_HEADS = 8
DIFF_HEAD_DIM = 64
N_EXPERTS = 8
LANES = 128
NEG_BIG = -1e30
VMEM_LIMIT = 56 * 1024 * 1024
DMA_UNROLL = 8
LOG2_E = 1.4426950408889634

SH1, SC1, G1, SH2, SC2, G2 = range(6)


def _cparams(sem):
    return pltpu.CompilerParams(dimension_semantics=sem, vmem_limit_bytes=VMEM_LIMIT)


def _rms_rows(x, gain):
    return x * lax.rsqrt(jnp.mean(x * x, axis=-1, keepdims=True) + EPS) * gain


def _silu(x):
    return x * jax.nn.sigmoid(x)


def _mod_kernel(c_ref, w_ref, b_ref, o_ref):
    a = _silu(c_ref[...])
    o_ref[...] = jnp.dot(a, w_ref[...], preferred_element_type=F32,
                         precision=lax.Precision.HIGHEST) + b_ref[...]


def _modulation(cc, w_mod, b_mod):
    r, d = cc.shape
    n = w_mod.shape[1]
    tn = n // 6
    out = pl.pallas_call(
        _mod_kernel,
        grid=(n // tn,),
        in_specs=[pl.BlockSpec((r, d), lambda j: (0, 0)),
                  pl.BlockSpec((d, tn), lambda j: (0, j)),
                  pl.BlockSpec((1, tn), lambda j: (0, j))],
        out_specs=pl.BlockSpec((r, tn), lambda j: (0, j)),
        out_shape=jax.ShapeDtypeStruct((r, n), F32),
        compiler_params=_cparams(("arbitrary",)),
        name="mod",
    )(cc, w_mod, b_mod.reshape(1, n))
    return out.reshape(r, 6, d)


def _inproj_kernel(x_ref, g_ref, mod_ref, w_ref, tab_ref, o_ref, h_scr, *, rope_scales, shifts, period):
    j = pl.program_id(2)

    @pl.when(j == 0)
    def _():
        y = _rms_rows(x_ref[0], g_ref[...])
        h = y * (1.0 + mod_ref[0, SC1:SC1 + 1, :]) + mod_ref[0, SH1:SH1 + 1, :]
        h_scr[...] = h.astype(BF16)

    acc = jnp.dot(h_scr[...], w_ref[...], preferred_element_type=F32)
    tn = acc.shape[1]

    def rope_store(scale):
        for k in range(tn // LANES):
            xs = acc[:, k * LANES:(k + 1) * LANES]
            p = (k * LANES) % period
            r = xs * tab_ref[0, :, p:p + LANES]
            for si, sft in enumerate(shifts):
                r = r + pltpu.roll(xs, sft, 1) * tab_ref[1 + si, :, p:p + LANES]
            if scale != 1.0:
                r = r * scale
            o_ref[0, :, k * LANES:(k + 1) * LANES] = r.astype(o_ref.dtype)

    for jj, scale in enumerate(rope_scales):
        pl.when(j == jj)(functools.partial(rope_store, scale))

    @pl.when(j >= len(rope_scales))
    def _():
        o_ref[0] = acc.astype(o_ref.dtype)


def _inproj(x, gain, mod, mod_row, w, tab, tab_per_tile, tm, tn, rope_scales, shifts):
    b, t, d = x.shape
    n = w.shape[1]
    period = tab.shape[2]
    kern = functools.partial(_inproj_kernel, rope_scales=rope_scales, shifts=shifts, period=period)
    tab_map = (lambda bb, i, j: (0, i, 0)) if tab_per_tile else (lambda bb, i, j: (0, 0, 0))
    return pl.pallas_call(
        kern,
        grid=(b, t // tm, n // tn),
        in_specs=[pl.BlockSpec((1, tm, d), lambda bb, i, j: (bb, i, 0)),
                  pl.BlockSpec((1, d), lambda bb, i, j: (0, 0)),
                  pl.BlockSpec((1, 6, d), lambda bb, i, j: (mod_row(bb), 0, 0)),
                  pl.BlockSpec((d, tn), lambda bb, i, j: (0, j)),
                  pl.BlockSpec((tab.shape[0], tm, period), tab_map)],
        out_specs=pl.BlockSpec((1, tm, tn), lambda bb, i, j: (bb, i, j)),
        out_shape=jax.ShapeDtypeStruct((b, t, n), BF16),
        scratch_shapes=[pltpu.VMEM((tm, d), BF16)],
        compiler_params=_cparams(("arbitrary", "arbitrary", "arbitrary")),
        name="inproj",
    )(x, gain.reshape(1, d), mod, w, tab)


def _ret_kernel(qf, kf, vf, gf, qb, kb, vb, gb, qc, kc, vc, gfc, gbc, intra, qdec, kdec, cdec,
                yf, yb, yfc, ybc, st):
    s = pl.program_id(2)

    def direction(d, q_ref, k_ref, v_ref, g_ref, y_ref):
        q = q_ref[0]
        k = k_ref[0]
        v = v_ref[0]
        scores = lax.dot_general(q, k, (((1,), (1,)), ((), ())), preferred_element_type=F32) * intra[d, 0]
        state = st[d]
        o = (jnp.dot(scores.astype(BF16), v, preferred_element_type=F32)
             + qdec[d, 0] * jnp.dot(q, state.astype(BF16), preferred_element_type=F32))
        ks = (k.astype(F32) * kdec[d, 0]).astype(BF16)
        st[d] = state * cdec[d, 0] + lax.dot_general(ks, v, (((0,), (0,)), ((), ())),
                                                     preferred_element_type=F32)
        hn = o * lax.rsqrt(jnp.mean(o * o, axis=-1, keepdims=True) + EPS)
        y_ref[0] = (_silu(g_ref[0].astype(F32)) * hn).astype(y_ref.dtype)

    @pl.when(s == 0)
    def _():
        st[...] = jnp.zeros(st.shape, F32)
        direction(0, qc, kc, vc, gfc, yfc)
        direction(1, qc, kc, vc, gbc, ybc)

    @pl.when(s > 0)
    def _():
        direction(0, qf, kf, vf, gf, yf)
        direction(1, qb, kb, vb, gb, yb)


def _retention(z, zc, decay_logit):
    b, t, _ = z.shape
    chunk = zc.shape[1]
    n_lat = t // chunk
    h = RET_HEADS
    dk = z.shape[2] // (8 * h)
    dv = 2 * dk

    log_g = jax.nn.log_sigmoid(decay_logit.astype(F32))
    idx = jnp.arange(chunk, dtype=F32)
    diff = idx[:, None] - idx[None, :]
    lg = log_g[:, :, None, None]
    intra_f = jnp.where(diff >= 0, jnp.exp(lg[0] * jnp.maximum(diff, 0.0)), 0.0)
    intra_b = jnp.where(diff <= 0, jnp.exp(lg[1] * jnp.maximum(-diff, 0.0)), 0.0)
    intra = jnp.stack([intra_f, intra_b])
    qdec = jnp.stack([jnp.exp(log_g[0][:, None] * (idx + 1.0)),
                      jnp.exp(log_g[1][:, None] * (chunk - idx))])[..., None]
    kdec = jnp.stack([jnp.exp(log_g[0][:, None] * (chunk - 1.0 - idx)),
                      jnp.exp(log_g[1][:, None] * idx)])[..., None]
    cdec = jnp.exp(log_g * chunk)[..., None, None]

    def fwd(s):
        return jnp.maximum(s - 1, 0)

    def bwd(s):
        return n_lat - jnp.maximum(s, 1)

    def spec(width, col0, chunk_of):
        return pl.BlockSpec((1, chunk, width), lambda bb, hh, s: (bb, chunk_of(s), col0 + hh))

    def cspec(width, col0):
        return pl.BlockSpec((1, chunk, width), lambda bb, hh, s: (bb, 0, col0 + hh))

    in_specs = [spec(dk, 0, fwd), spec(dk, h, fwd), spec(dv, h, fwd), spec(dv, 2 * h, fwd),
                spec(dk, 0, bwd), spec(dk, h, bwd), spec(dv, h, bwd), spec(dv, 3 * h, bwd),
                cspec(dk, 0), cspec(dk, h), cspec(dv, h), cspec(dv, 2 * h), cspec(dv, 3 * h),
                pl.BlockSpec((2, 1, chunk, chunk), lambda bb, hh, s: (0, hh, 0, 0)),
                pl.BlockSpec((2, 1, chunk, 1), lambda bb, hh, s: (0, hh, 0, 0)),
                pl.BlockSpec((2, 1, chunk, 1), lambda bb, hh, s: (0, hh, 0, 0)),
                pl.BlockSpec((2, 1, 1, 1), lambda bb, hh, s: (0, hh, 0, 0))]
    out_specs = [pl.BlockSpec((1, chunk, dv), lambda bb, hh, s: (bb, fwd(s), hh)),
                 pl.BlockSpec((1, chunk, dv), lambda bb, hh, s: (bb, bwd(s), hh)),
                 pl.BlockSpec((1, chunk, dv), lambda bb, hh, s: (bb, 0, hh)),
                 pl.BlockSpec((1, chunk, dv), lambda bb, hh, s: (bb, 0, hh))]
    yl = jax.ShapeDtypeStruct((b, t, h * dv), BF16)
    yc = jax.ShapeDtypeStruct((b, chunk, h * dv), BF16)
    return pl.pallas_call(
        _ret_kernel,
        grid=(b, h, n_lat + 1),
        in_specs=in_specs,
        out_specs=out_specs,
        out_shape=[yl, yl, yc, yc],
        scratch_shapes=[pltpu.VMEM((2, dk, dv), F32)],
        compiler_params=_cparams(("arbitrary", "arbitrary", "arbitrary")),
        name="retention",
    )(z, z, z, z, z, z, z, z, zc, zc, zc, zc, zc, intra, qdec, kdec, cdec)


def _pack_rows(h):
    n = h.shape[1] // 2
    bits = lax.bitcast_convert_type(h.astype(BF16).astype(F32), U32)
    return (bits[:, :n] >> 16) | (bits[:, n:] & jnp.uint32(0xFFFF0000))


def _unpack_rows(w):
    lo = lax.bitcast_convert_type(w << 16, F32)
    hi = lax.bitcast_convert_type(w & jnp.uint32(0xFFFF0000), F32)
    return lo, hi


def _outproj_kernel(*refs, n_y, route):
    y_refs = refs[:n_y]
    w_ref, x_ref, gpost_ref, gpre_ref, mod_ref = refs[n_y:n_y + 5]
    rest = refs[n_y + 5:]
    if route:
        rw_ref, rb_ref, xo_ref, hp_ref, rt_ref = rest
    else:
        xo_ref, h_ref = rest
    y = y_refs[0][0]
    if n_y == 2:
        y = (y.astype(F32) + y_refs[1][0].astype(F32)).astype(BF16)
    o = jnp.dot(y, w_ref[...], preferred_element_type=F32)
    xn = x_ref[0] + mod_ref[0, G1:G1 + 1, :] * _rms_rows(o, gpost_ref[...])
    xo_ref[0] = xn
    h = _rms_rows(xn, gpre_ref[...]) * (1.0 + mod_ref[0, SC2:SC2 + 1, :]) + mod_ref[0, SH2:SH2 + 1, :]
    if not route:
        h_ref[0] = h.astype(h_ref.dtype)
        return
    hp_ref[0] = _pack_rows(h)
    logits = jnp.dot(h, rw_ref[...], preferred_element_type=F32,
                     precision=lax.Precision.HIGHEST) + rb_ref[...]
    lane = lax.broadcasted_iota(I32, logits.shape, 1).astype(F32)
    m1 = jnp.max(logits, axis=-1, keepdims=True)
    i1 = jnp.min(jnp.where(logits == m1, lane, float(LANES)), axis=-1, keepdims=True)
    masked = jnp.where(lane == i1, NEG_BIG, logits)
    m2 = jnp.max(masked, axis=-1, keepdims=True)
    i2 = jnp.min(jnp.where(masked == m2, lane, float(LANES)), axis=-1, keepdims=True)
    e2 = jnp.exp(m2 - m1)
    den = 1.0 + e2
    w1 = 1.0 / den
    w2 = e2 / den
    rt_ref[0] = jnp.where(lane == 0.0, i1,
                          jnp.where(lane == 1.0, i2,
                                    jnp.where(lane == 2.0, w1, jnp.where(lane == 3.0, w2, 0.0))))


def _outproj(ys, w, x, gpost, gpre, mod, mod_row, tm, router=None):
    b, t, d = x.shape
    kdim = w.shape[0]
    route = router is not None
    in_specs = [pl.BlockSpec((1, tm, kdim), lambda bb, i: (bb, i, 0)) for _ in ys]
    in_specs += [pl.BlockSpec((kdim, d), lambda bb, i: (0, 0)),
                 pl.BlockSpec((1, tm, d), lambda bb, i: (bb, i, 0)),
                 pl.BlockSpec((1, d), lambda bb, i: (0, 0)),
                 pl.BlockSpec((1, d), lambda bb, i: (0, 0)),
                 pl.BlockSpec((1, 6, d), lambda bb, i: (mod_row(bb), 0, 0))]
    args = list(ys) + [w, x, gpost.reshape(1, d), gpre.reshape(1, d), mod]
    out_specs = [pl.BlockSpec((1, tm, d), lambda bb, i: (bb, i, 0))]
    out_shape = [jax.ShapeDtypeStruct((b, t, d), F32)]
    if route:
        rw, rb = router
        in_specs += [pl.BlockSpec((d, LANES), lambda bb, i: (0, 0)),
                     pl.BlockSpec((1, LANES), lambda bb, i: (0, 0))]
        args += [rw, rb]
        out_specs += [pl.BlockSpec((1, tm, d // 2), lambda bb, i: (bb, i, 0)),
                      pl.BlockSpec((1, tm, LANES), lambda bb, i: (bb, i, 0))]
        out_shape += [jax.ShapeDtypeStruct((b, t, d // 2), U32),
                      jax.ShapeDtypeStruct((b, t, LANES), F32)]
    else:
        out_specs.append(pl.BlockSpec((1, tm, d), lambda bb, i: (bb, i, 0)))
        out_shape.append(jax.ShapeDtypeStruct((b, t, d), BF16))
    return pl.pallas_call(
        functools.partial(_outproj_kernel, n_y=len(ys), route=route),
        grid=(b, t // tm),
        in_specs=in_specs,
        out_specs=out_specs,
        out_shape=out_shape,
        compiler_params=_cparams(("arbitrary", "arbitrary")),
        name="outproj",
    )(*args)


def _ffn_kernel(h_ref, wg_ref, wu_ref, wd_ref, x_ref, gpost_ref, mod_ref, o_ref, acc):
    f = pl.program_id(2)
    h = h_ref[0]
    g = jnp.dot(h, wg_ref[...], preferred_element_type=F32)
    u = jnp.dot(h, wu_ref[...], preferred_element_type=F32)
    a = (_silu(g) * u).astype(BF16)
    part = jnp.dot(a, wd_ref[...], preferred_element_type=F32)

    @pl.when(f == 0)
    def _():
        acc[...] = part

    @pl.when(f > 0)
    def _():
        acc[...] += part

    @pl.when(f == pl.num_programs(2) - 1)
    def _():
        o_ref[0] = x_ref[0] + mod_ref[0, G2:G2 + 1, :] * _rms_rows(acc[...], gpost_ref[...])


def _ffn(h, w_gu, w_down, x, gpost, mod, mod_row, tm, nf):
    b, t, d = x.shape
    dff = w_down.shape[0]
    tf = dff // nf
    return pl.pallas_call(
        _ffn_kernel,
        grid=(b, t // tm, nf),
        in_specs=[pl.BlockSpec((1, tm, d), lambda bb, i, f: (bb, i, 0)),
                  pl.BlockSpec((d, tf), lambda bb, i, f: (0, f)),
                  pl.BlockSpec((d, tf), lambda bb, i, f: (0, f + nf)),
                  pl.BlockSpec((tf, d), lambda bb, i, f: (f, 0)),
                  pl.BlockSpec((1, tm, d), lambda bb, i, f: (bb, i, 0)),
                  pl.BlockSpec((1, d), lambda bb, i, f: (0, 0)),
                  pl.BlockSpec((1, 6, d), lambda bb, i, f: (mod_row(bb), 0, 0))],
        out_specs=pl.BlockSpec((1, tm, d), lambda bb, i, f: (bb, i, 0)),
        out_shape=jax.ShapeDtypeStruct((b, t, d), F32),
        scratch_shapes=[pltpu.VMEM((tm, d), F32)],
        compiler_params=_cparams(("arbitrary", "arbitrary", "arbitrary")),
        name="ffn",
    )(h, w_gu, w_gu, w_down, x, gpost.reshape(1, d), mod)


def _attn_kernel(q_ref, kl_ref, vl_ref, kc_ref, vc_ref, lam_ref, subln_ref, o_ref, ks, vs, *, lambda_init):
    t = kl_ref.shape[1]

    @pl.when(pl.program_id(2) == 0)
    def _():
        ks[0:t, :] = kl_ref[0]
        ks[t:, :] = kc_ref[0]
        vs[0:t, 0:LANES] = vl_ref[0]
        vs[t:, 0:LANES] = vc_ref[0]
        vs[:, LANES:] = jnp.ones((vs.shape[0], LANES), BF16)

    q = q_ref[0]
    k = ks[...]
    v = vs[...]
    lane = lax.broadcasted_iota(I32, q.shape, 1)
    zero = jnp.zeros_like(q)
    lp = lam_ref[...]
    lam = (jnp.exp(jnp.sum(lp[0:1] * lp[1:2], axis=-1, keepdims=True))
           - jnp.exp(jnp.sum(lp[2:3] * lp[3:4], axis=-1, keepdims=True)) + lambda_init)

    def one(qh):
        s = lax.dot_general(qh, k, (((1,), (1,)), ((), ())), preferred_element_type=F32)
        m = jnp.max(s, axis=-1, keepdims=True)
        p = jnp.exp2(s - m).astype(BF16)
        r = jnp.dot(p, v, preferred_element_type=F32)
        return r[:, 0:LANES] / r[:, LANES:LANES + 1]

    o = one(jnp.where(lane < DIFF_HEAD_DIM, q, zero)) - lam * one(jnp.where(lane >= DIFF_HEAD_DIM, q, zero))
    o_ref[0] = (_rms_rows(o, subln_ref[...]) * (1.0 - lambda_init)).astype(o_ref.dtype)


def _attention(z, zc, lam_params, subln, tq, lambda_init):
    b, t, n3 = z.shape
    tc = zc.shape[1]
    d = n3 // 3
    npair = d // LANES
    return pl.pallas_call(
        functools.partial(_attn_kernel, lambda_init=lambda_init),
        grid=(b, npair, t // tq),
        in_specs=[pl.BlockSpec((1, tq, LANES), lambda bb, p, i: (bb, i, p)),
                  pl.BlockSpec((1, t, LANES), lambda bb, p, i: (bb, 0, npair + p)),
                  pl.BlockSpec((1, t, LANES), lambda bb, p, i: (bb, 0, 2 * npair + p)),
                  pl.BlockSpec((1, tc, LANES), lambda bb, p, i: (bb, 0, npair + p)),
                  pl.BlockSpec((1, tc, LANES), lambda bb, p, i: (bb, 0, 2 * npair + p)),
                  pl.BlockSpec(lam_params.shape, lambda bb, p, i: (0, 0)),
                  pl.BlockSpec((1, LANES), lambda bb, p, i: (0, 0))],
        out_specs=pl.BlockSpec((1, tq, LANES), lambda bb, p, i: (bb, i, p)),
        out_shape=jax.ShapeDtypeStruct((b, t, d), BF16),
        scratch_shapes=[pltpu.VMEM((t + tc, LANES), BF16), pltpu.VMEM((t + tc, 2 * LANES), BF16)],
        compiler_params=_cparams(("arbitrary", "arbitrary", "arbitrary")),
        name="attention",
    )(z, z, z, zc, zc, lam_params.astype(F32), subln.reshape(1, LANES).astype(F32))


def _moe_kernel(te_ref, tv_ref, inv_ref, ninv_ref, wg_ref, wu_ref, wd_ref, h_hbm,
                y_hbm, gbuf, hbuf, acc, obuf, gsem, ssem, *, tm, nt, n_tok):
    del te_ref
    j = pl.program_id(0)
    f = pl.program_id(1)
    nf = pl.num_programs(1)
    slot = j % 2
    valid = tv_ref[j] > 0
    next_valid = tv_ref[jnp.minimum(j + 1, nt - 1)] * (j + 1 < nt).astype(I32) > 0

    def start_gather(idx_ref, to_slot):
        def body(r, c):
            tok = jnp.minimum(idx_ref[0, 0, r] >> 1, n_tok - 1)
            pltpu.make_async_copy(h_hbm.at[pl.ds(tok, 1)], gbuf.at[to_slot, pl.ds(r, 1)],
                                  gsem.at[to_slot]).start()
            return c
        lax.fori_loop(0, tm, body, 0, unroll=DMA_UNROLL)

    def wait_scatter():
        pltpu.make_async_copy(obuf, y_hbm.at[pl.ds(0, tm)], ssem.at[0]).wait()

    @pl.when(jnp.logical_and(j == 0, f == 0))
    def _():
        obuf[...] = jnp.zeros(obuf.shape, U32)
        cp = pltpu.make_async_copy(obuf, y_hbm.at[pl.ds(2 * n_tok, tm)], ssem.at[0])
        cp.start()
        cp.wait()

    @pl.when(jnp.logical_and(valid, f == 0))
    def _():
        @pl.when(j == 0)
        def _():
            start_gather(inv_ref, 0)

        pltpu.make_async_copy(h_hbm.at[pl.ds(0, tm)], gbuf.at[slot], gsem.at[slot]).wait()
        lo, hi = _unpack_rows(gbuf[slot])
        half = lo.shape[1]
        hbuf[:, :half] = lo.astype(BF16)
        hbuf[:, half:] = hi.astype(BF16)

        @pl.when(next_valid)
        def _():
            start_gather(ninv_ref, 1 - slot)

    @pl.when(valid)
    def _():
        h = hbuf[...]
        g = jnp.dot(h, wg_ref[0], preferred_element_type=F32)
        u = jnp.dot(h, wu_ref[0], preferred_element_type=F32)
        a = (_silu(g) * u).astype(BF16)
        part = jnp.dot(a, wd_ref[0], preferred_element_type=F32)

        @pl.when(f == 0)
        def _():
            acc[...] = part

        @pl.when(f > 0)
        def _():
            acc[...] += part

    @pl.when(jnp.logical_and(valid, f == nf - 1))
    def _():
        @pl.when(j > 0)
        def _():
            wait_scatter()

        obuf[...] = _pack_rows(acc[...])

        def body(r, c):
            a = inv_ref[0, 0, r]
            row = jnp.where(a < 2 * n_tok, (a & 1) * n_tok + (a >> 1), 2 * n_tok + r)
            pltpu.make_async_copy(obuf.at[pl.ds(r, 1)], y_hbm.at[pl.ds(row, 1)], ssem.at[0]).start()
            return c
        lax.fori_loop(0, tm, body, 0, unroll=DMA_UNROLL)

        @pl.when(jnp.logical_not(next_valid))
        def _():
            wait_scatter()


def _moe(hp, te, tv, inv, w_gu, w_down, tm, tf):
    nt = te.shape[0]
    n_tok, half = hp.shape
    e, d, two_ff = w_gu.shape
    dff = two_ff // 2
    nf = dff // tf

    def wf(j, f, te_r, tv_r):
        return jnp.where(tv_r[j] > 0, f, nf - 1)

    grid_spec = pltpu.PrefetchScalarGridSpec(
        num_scalar_prefetch=2,
        grid=(nt, nf),
        in_specs=[pl.BlockSpec((1, 1, tm), lambda j, f, a, c: (j, 0, 0), memory_space=pltpu.SMEM),
                  pl.BlockSpec((1, 1, tm), lambda j, f, a, c: (jnp.minimum(j + 1, nt - 1), 0, 0),
                               memory_space=pltpu.SMEM),
                  pl.BlockSpec((1, d, tf), lambda j, f, a, c: (a[j], 0, wf(j, f, a, c))),
                  pl.BlockSpec((1, d, tf), lambda j, f, a, c: (a[j], 0, wf(j, f, a, c) + nf)),
                  pl.BlockSpec((1, tf, d), lambda j, f, a, c: (a[j], wf(j, f, a, c), 0)),
                  pl.BlockSpec(memory_space=pl.ANY)],
        out_specs=pl.BlockSpec(memory_space=pl.ANY),
        scratch_shapes=[pltpu.VMEM((2, tm, half), U32),
                        pltpu.VMEM((tm, d), BF16),
                        pltpu.VMEM((tm, d), F32),
                        pltpu.VMEM((tm, half), U32),
                        pltpu.SemaphoreType.DMA((2,)),
                        pltpu.SemaphoreType.DMA((1,))],
    )
    return pl.pallas_call(
        functools.partial(_moe_kernel, tm=tm, nt=nt, n_tok=n_tok),
        grid_spec=grid_spec,
        out_shape=jax.ShapeDtypeStruct((2 * n_tok + tm, half), U32),
        compiler_params=_cparams(("arbitrary", "arbitrary")),
        name="moe",
    )(te, tv, inv, inv, w_gu, w_gu, w_down, hp)


def _route_plan(rt, tm, nt):
    t = rt.shape[0]
    e_flat = rt[:, 0:2].astype(I32).reshape(-1)
    onehot = (e_flat[:, None] == jnp.arange(N_EXPERTS, dtype=I32)[None, :]).astype(I32)
    csum = jnp.cumsum(onehot, axis=0)
    rank = jnp.sum(csum * onehot, axis=1) - 1
    counts = csum[-1]
    tiles_e = (counts + tm - 1) // tm
    tile_end = jnp.cumsum(tiles_e)
    row_off = (tile_end - tiles_e) * tm
    dest = jnp.sum(row_off[None, :] * onehot, axis=1) + rank
    inv = jnp.full((nt * tm,), 2 * t, I32).at[dest].set(jnp.arange(2 * t, dtype=I32))
    n_used = tile_end[-1]
    tid = jnp.arange(nt, dtype=I32)
    tv = (tid < n_used).astype(I32)
    te = jnp.sum((jnp.minimum(tid, n_used - 1)[:, None] >= tile_end[None, :]).astype(I32), axis=1)
    return jnp.minimum(te, N_EXPERTS - 1), tv, inv.reshape(nt, 1, tm)


def _combine_kernel(y0_ref, y1_ref, rt_ref, x_ref, gpost_ref, mod_ref, o_ref):
    lo0, hi0 = _unpack_rows(y0_ref[...])
    lo1, hi1 = _unpack_rows(y1_ref[...])
    w0 = rt_ref[0, :, 2:3]
    w1 = rt_ref[0, :, 3:4]
    y = jnp.concatenate([w0 * lo0 + w1 * lo1, w0 * hi0 + w1 * hi1], axis=1)
    o_ref[0] = x_ref[0] + mod_ref[0, G2:G2 + 1, :] * _rms_rows(y, gpost_ref[...])


def _combine(y2, rt, x, gpost, mod, tm):
    b, t, d = x.shape
    per_b = t // tm
    nblk = b * per_b
    return pl.pallas_call(
        _combine_kernel,
        grid=(b, per_b),
        in_specs=[pl.BlockSpec((tm, d // 2), lambda bb, i: (bb * per_b + i, 0)),
                  pl.BlockSpec((tm, d // 2), lambda bb, i: (nblk + bb * per_b + i, 0)),
                  pl.BlockSpec((1, tm, LANES), lambda bb, i: (bb, i, 0)),
                  pl.BlockSpec((1, tm, d), lambda bb, i: (bb, i, 0)),
                  pl.BlockSpec((1, d), lambda bb, i: (0, 0)),
                  pl.BlockSpec((1, 6, d), lambda bb, i: (bb, 0, 0))],
        out_specs=pl.BlockSpec((1, tm, d), lambda bb, i: (bb, i, 0)),
        out_shape=jax.ShapeDtypeStruct((b, t, d), F32),
        compiler_params=_cparams(("arbitrary", "arbitrary")),
        name="combine",
    )(y2, y2, rt, x, gpost.reshape(1, d), mod)


def _rope_angles(rows, head_dim):
    pos_r = jnp.repeat(jnp.arange(rows, dtype=F32), GRID_W)
    pos_c = jnp.tile(jnp.arange(GRID_W, dtype=F32), rows)
    axis_dim = head_dim // 2
    freqs = ROPE_BASE ** (-jnp.arange(0, axis_dim, 2, dtype=F32) / axis_dim)
    return pos_r[:, None] * freqs[None, :], pos_c[:, None] * freqs[None, :]


def _rope_tables(rows, head_dim, reps):
    ar, ac = _rope_angles(rows, head_dim)
    cos = jnp.concatenate([jnp.cos(ar), jnp.cos(ar), jnp.cos(ac), jnp.cos(ac)], axis=1)
    zr = jnp.zeros_like(ar)
    s_up = jnp.concatenate([-jnp.sin(ar), zr, -jnp.sin(ac), zr], axis=1)
    s_dn = jnp.concatenate([zr, jnp.sin(ar), zr, jnp.sin(ac)], axis=1)
    return jnp.stack([jnp.tile(cos, (1, reps)), jnp.tile(s_up, (1, reps)), jnp.tile(s_dn, (1, reps))])


def _identity_tables(n_tab, tm, period):
    return jnp.concatenate([jnp.ones((1, tm, period), F32), jnp.zeros((n_tab - 1, tm, period), F32)])


def _tile(t, want):
    return min(t, want)


def kernel(x, c, ctx, c_ctx, l0_w_mod, l0_b_mod, l0_norm_pre_mix, l0_norm_post_mix, l0_norm_pre_ffn, l0_norm_post_ffn, l0_ret_w_in, l0_ret_decay_logit, l0_ret_w_out, l0_ffn_w_gate_up, l0_ffn_w_down, l1_w_mod, l1_b_mod, l1_norm_pre_mix, l1_norm_post_mix, l1_norm_pre_ffn, l1_norm_post_ffn, l1_attn_w_in, l1_attn_lambda, l1_attn_subln, l1_attn_w_out, l1_router_w, l1_router_b, l1_moe_w_gate_up, l1_moe_w_down):
    b, t, d = x.shape
    tc = ctx.shape[1]
    rows = t // GRID_W
    assert t % tc == 0

    n_rows = -(-(b + 1) // 8) * 8
    cc = jnp.concatenate([c, c_ctx[None, :], jnp.zeros((n_rows - b - 1, d), F32)], axis=0)
    lat_row = lambda bb: bb
    ctx_row = lambda bb: b

    tm_lat = _tile(t, 1024)
    tn = 1024

    mod0 = _modulation(cc, l0_w_mod, l0_b_mod)
    w_in0 = l0_ret_w_in.astype(BF16)
    dk = w_in0.shape[1] // (8 * RET_HEADS)
    tab_ret = _rope_tables(rows, dk, 1)
    tab_ret = jnp.stack([tab_ret[0], tab_ret[1] + tab_ret[2]])
    ret_scales = (1.0,) * (RET_HEADS * dk // tn) + (dk ** -0.5,) * (RET_HEADS * dk // tn)
    z0 = _inproj(x, l0_norm_pre_mix, mod0, lat_row, w_in0, tab_ret, True, tm_lat, tn, ret_scales, (dk // 4,))
    zc0 = _inproj(ctx, l0_norm_pre_mix, mod0, ctx_row, w_in0, _identity_tables(2, tc, dk), False, tc, tn,
                  ret_scales, (dk // 4,))
    yf, yb, yfc, ybc = _retention(z0, zc0, l0_ret_decay_logit)

    w_out0 = l0_ret_w_out.astype(BF16)
    tm_o = _tile(t, 512)
    x1, h1 = _outproj([yf, yb], w_out0, x, l0_norm_post_mix, l0_norm_pre_ffn, mod0, lat_row, tm_o)
    c1, hc1 = _outproj([yfc, ybc], w_out0, ctx, l0_norm_post_mix, l0_norm_pre_ffn, mod0, ctx_row, tc)

    w_gu0 = l0_ffn_w_gate_up.astype(BF16)
    w_dn0 = l0_ffn_w_down.astype(BF16)
    x2 = _ffn(h1, w_gu0, w_dn0, x1, l0_norm_post_ffn, mod0, lat_row, tm_lat, 2)
    c2 = _ffn(hc1, w_gu0, w_dn0, c1, l0_norm_post_ffn, mod0, ctx_row, tc, 2)

    lambda_init = 0.8 - 0.6 * math.exp(-0.3 * 1)
    mod1 = _modulation(cc, l1_w_mod, l1_b_mod)
    w_in1 = l1_attn_w_in.astype(BF16)
    tab_diff = _rope_tables(rows, DIFF_HEAD_DIM, LANES // DIFF_HEAD_DIM)
    q4 = DIFF_HEAD_DIM // 4
    diff_scales = (DIFF_HEAD_DIM ** -0.5 * LOG2_E,) * (d // tn) + (1.0,) * (d // tn)
    z1 = _inproj(x2, l1_norm_pre_mix, mod1, lat_row, w_in1, tab_diff, True, tm_lat, tn, diff_scales,
                 (LANES - q4, q4))
    zc1 = _inproj(c2, l1_norm_pre_mix, mod1, ctx_row, w_in1, _identity_tables(3, tc, LANES), False, tc, tn,
                  (1.0,) * (2 * d // tn), (LANES - q4, q4))
    o1 = _attention(z1, zc1, l1_attn_lambda, l1_attn_subln, _tile(t, 256), lambda_init)

    rw = jnp.zeros((d, LANES), F32).at[:, :N_EXPERTS].set(l1_router_w)
    rb = jnp.full((1, LANES), NEG_BIG, F32).at[0, :N_EXPERTS].set(l1_router_b)
    x3, hp, rt = _outproj([o1], l1_attn_w_out.astype(BF16), x2, l1_norm_post_mix, l1_norm_pre_ffn, mod1,
                          lat_row, tm_o, router=(rw, rb))

    n_tok = b * t
    tm_e = _tile(n_tok, 1024)
    nt = 2 * n_tok // tm_e + N_EXPERTS
    te, tv, inv = _route_plan(rt.reshape(n_tok, LANES), tm_e, nt)
    dff_e = l1_moe_w_down.shape[1]
    y2 = _moe(hp.reshape(n_tok, d // 2), te, tv, inv, l1_moe_w_gate_up.astype(BF16),
              l1_moe_w_down.astype(BF16), tm_e, dff_e // 7)
    return _combine(y2, rt, x3, l1_norm_post_ffn, mod1, tm_o)
```

```python
import functools
import math

import jax
import jax.numpy as jnp
from jax import lax
from jax.experimental import pallas as pl
from jax.experimental.pallas import tpu as pltpu

F32 = jnp.float32
BF16 = jnp.bfloat16
U32 = jnp.uint32
I32 = jnp.int32

EPS = 1e-6
GRID_W = 64
ROPE_BASE = 10000.0
RET_HEADS = 4
DIFF_HEADS = 8
DIFF_HEAD_DIM = 64
N_EXPERTS = 8
LANES = 128
NEG_BIG = -1e30
VMEM_LIMIT = 56 * 1024 * 1024
DMA_UNROLL = 8
MOE_HIDDEN_STEPS = 4
OUTPROJ_ROW_GROUP = 256
LOG2_E = 1.4426950408889634

SH1, SC1, G1, SH2, SC2, G2 = range(6)


def _cparams(sem):
    return pltpu.CompilerParams(dimension_semantics=sem, vmem_limit_bytes=VMEM_LIMIT)


def _rms_rows(x, gain):
    return x * lax.rsqrt(jnp.mean(x * x, axis=-1, keepdims=True) + EPS) * gain


def _silu(x):
    return x * jax.nn.sigmoid(x)


def _mod_kernel(c_ref, w_ref, b_ref, o_ref):
    a = _silu(c_ref[...])
    o_ref[...] = jnp.dot(a, w_ref[...], preferred_element_type=F32,
                         precision=lax.Precision.HIGHEST) + b_ref[...]


def _modulation(cc, w_mod, b_mod):
    r, d = cc.shape
    n = w_mod.shape[1]
    tn = n // 6
    out = pl.pallas_call(
        _mod_kernel,
        grid=(n // tn,),
        in_specs=[pl.BlockSpec((r, d), lambda j: (0, 0)),
                  pl.BlockSpec((d, tn), lambda j: (0, j)),
                  pl.BlockSpec((1, tn), lambda j: (0, j))],
        out_specs=pl.BlockSpec((r, tn), lambda j: (0, j)),
        out_shape=jax.ShapeDtypeStruct((r, n), F32),
        compiler_params=_cparams(("arbitrary",)),
        name="mod",
    )(cc, w_mod, b_mod.reshape(1, n))
    return out.reshape(r, 6, d)


def _inproj_kernel(x_ref, g_ref, mod_ref, w_ref, tab_ref, o_ref, h_scr, *, rope_scales, shifts, period):
    j = pl.program_id(2)

    @pl.when(j == 0)
    def _():
        y = _rms_rows(x_ref[0], g_ref[...])
        h = y * (1.0 + mod_ref[0, SC1:SC1 + 1, :]) + mod_ref[0, SH1:SH1 + 1, :]
        h_scr[...] = h.astype(BF16)

    acc = jnp.dot(h_scr[...], w_ref[...], preferred_element_type=F32)
    tn = acc.shape[1]

    def rope_store(scale):
        for k in range(tn // LANES):
            xs = acc[:, k * LANES:(k + 1) * LANES]
            p = (k * LANES) % period
            r = xs * tab_ref[0, :, p:p + LANES]
            for si, sft in enumerate(shifts):
                r = r + pltpu.roll(xs, sft, 1) * tab_ref[1 + si, :, p:p + LANES]
            if scale != 1.0:
                r = r * scale
            o_ref[0, :, k * LANES:(k + 1) * LANES] = r.astype(o_ref.dtype)

    for jj, scale in enumerate(rope_scales):
        pl.when(j == jj)(functools.partial(rope_store, scale))

    @pl.when(j >= len(rope_scales))
    def _():
        o_ref[0] = acc.astype(o_ref.dtype)


def _inproj(x, gain, mod, mod_row, w, tab, tab_per_tile, tm, tn, rope_scales, shifts):
    b, t, d = x.shape
    n = w.shape[1]
    period = tab.shape[2]
    kern = functools.partial(_inproj_kernel, rope_scales=rope_scales, shifts=shifts, period=period)
    tab_map = (lambda bb, i, j: (0, i, 0)) if tab_per_tile else (lambda bb, i, j: (0, 0, 0))
    return pl.pallas_call(
        kern,
        grid=(b, t // tm, n // tn),
        in_specs=[pl.BlockSpec((1, tm, d), lambda bb, i, j: (bb, i, 0)),
                  pl.BlockSpec((1, d), lambda bb, i, j: (0, 0)),
                  pl.BlockSpec((1, 6, d), lambda bb, i, j: (mod_row(bb), 0, 0)),
                  pl.BlockSpec((d, tn), lambda bb, i, j: (0, j)),
                  pl.BlockSpec((tab.shape[0], tm, period), tab_map)],
        out_specs=pl.BlockSpec((1, tm, tn), lambda bb, i, j: (bb, i, j)),
        out_shape=jax.ShapeDtypeStruct((b, t, n), BF16),
        scratch_shapes=[pltpu.VMEM((tm, d), BF16)],
        compiler_params=_cparams(("arbitrary", "arbitrary", "arbitrary")),
        name="inproj",
    )(x, gain.reshape(1, d), mod, w, tab)


def _ret_kernel(qf, kf, vf, gf, qb, kb, vb, gb, qc, kc, vc, gfc, gbc, intra, qdec, kdec, cdec,
                yf, yb, yfc, ybc, st):
    s = pl.program_id(2)

    def direction(d, q_ref, k_ref, v_ref, g_ref, y_ref):
        q = q_ref[0]
        k = k_ref[0]
        v = v_ref[0]
        scores = lax.dot_general(q, k, (((1,), (1,)), ((), ())), preferred_element_type=F32) * intra[d, 0]
        state = st[d]
        o = (jnp.dot(scores.astype(BF16), v, preferred_element_type=F32)
             + qdec[d, 0] * jnp.dot(q, state.astype(BF16), preferred_element_type=F32))
        ks = (k.astype(F32) * kdec[d, 0]).astype(BF16)
        st[d] = state * cdec[d, 0] + lax.dot_general(ks, v, (((0,), (0,)), ((), ())),
                                                     preferred_element_type=F32)
        hn = o * lax.rsqrt(jnp.mean(o * o, axis=-1, keepdims=True) + EPS)
        y_ref[0] = (_silu(g_ref[0].astype(F32)) * hn).astype(y_ref.dtype)

    @pl.when(s == 0)
    def _():
        st[...] = jnp.zeros(st.shape, F32)
        direction(0, qc, kc, vc, gfc, yfc)
        direction(1, qc, kc, vc, gbc, ybc)

    @pl.when(s > 0)
    def _():
        direction(0, qf, kf, vf, gf, yf)
        direction(1, qb, kb, vb, gb, yb)


def _retention(z, zc, decay_logit):
    b, t, _ = z.shape
    chunk = zc.shape[1]
    n_lat = t // chunk
    h = RET_HEADS
    dk = z.shape[2] // (8 * h)
    dv = 2 * dk

    log_g = jax.nn.log_sigmoid(decay_logit.astype(F32))
    idx = jnp.arange(chunk, dtype=F32)
    diff = idx[:, None] - idx[None, :]
    lg = log_g[:, :, None, None]
    intra_f = jnp.where(diff >= 0, jnp.exp(lg[0] * jnp.maximum(diff, 0.0)), 0.0)
    intra_b = jnp.where(diff <= 0, jnp.exp(lg[1] * jnp.maximum(-diff, 0.0)), 0.0)
    intra = jnp.stack([intra_f, intra_b])
    qdec = jnp.stack([jnp.exp(log_g[0][:, None] * (idx + 1.0)),
                      jnp.exp(log_g[1][:, None] * (chunk - idx))])[..., None]
    kdec = jnp.stack([jnp.exp(log_g[0][:, None] * (chunk - 1.0 - idx)),
                      jnp.exp(log_g[1][:, None] * idx)])[..., None]
    cdec = jnp.exp(log_g * chunk)[..., None, None]

    def fwd(s):
        return jnp.maximum(s - 1, 0)

    def bwd(s):
        return n_lat - jnp.maximum(s, 1)

    def spec(width, col0, chunk_of):
        return pl.BlockSpec((1, chunk, width), lambda bb, hh, s: (bb, chunk_of(s), col0 + hh))

    def cspec(width, col0):
        return pl.BlockSpec((1, chunk, width), lambda bb, hh, s: (bb, 0, col0 + hh))

    in_specs = [spec(dk, 0, fwd), spec(dk, h, fwd), spec(dv, h, fwd), spec(dv, 2 * h, fwd),
                spec(dk, 0, bwd), spec(dk, h, bwd), spec(dv, h, bwd), spec(dv, 3 * h, bwd),
                cspec(dk, 0), cspec(dk, h), cspec(dv, h), cspec(dv, 2 * h), cspec(dv, 3 * h),
                pl.BlockSpec((2, 1, chunk, chunk), lambda bb, hh, s: (0, hh, 0, 0)),
                pl.BlockSpec((2, 1, chunk, 1), lambda bb, hh, s: (0, hh, 0, 0)),
                pl.BlockSpec((2, 1, chunk, 1), lambda bb, hh, s: (0, hh, 0, 0)),
                pl.BlockSpec((2, 1, 1, 1), lambda bb, hh, s: (0, hh, 0, 0))]
    out_specs = [pl.BlockSpec((1, chunk, dv), lambda bb, hh, s: (bb, fwd(s), hh)),
                 pl.BlockSpec((1, chunk, dv), lambda bb, hh, s: (bb, bwd(s), hh)),
                 pl.BlockSpec((1, chunk, dv), lambda bb, hh, s: (bb, 0, hh)),
                 pl.BlockSpec((1, chunk, dv), lambda bb, hh, s: (bb, 0, hh))]
    yl = jax.ShapeDtypeStruct((b, t, h * dv), BF16)
    yc = jax.ShapeDtypeStruct((b, chunk, h * dv), BF16)
    return pl.pallas_call(
        _ret_kernel,
        grid=(b, h, n_lat + 1),
        in_specs=in_specs,
        out_specs=out_specs,
        out_shape=[yl, yl, yc, yc],
        scratch_shapes=[pltpu.VMEM((2, dk, dv), F32)],
        compiler_params=_cparams(("arbitrary", "arbitrary", "arbitrary")),
        name="retention",
    )(z, z, z, z, z, z, z, z, zc, zc, zc, zc, zc, intra, qdec, kdec, cdec)


def _pack_rows(h):
    n = h.shape[1] // 2
    bits = lax.bitcast_convert_type(h.astype(BF16).astype(F32), U32)
    return (bits[:, :n] >> 16) | (bits[:, n:] & jnp.uint32(0xFFFF0000))


def _unpack_rows(w):
    lo = lax.bitcast_convert_type(w << 16, F32)
    hi = lax.bitcast_convert_type(w & jnp.uint32(0xFFFF0000), F32)
    return lo, hi


def _outproj_kernel(*refs, n_y, route):
    y_refs = refs[:n_y]
    w_ref, x_ref, gpost_ref, gpre_ref, mod_ref = refs[n_y:n_y + 5]
    rest = refs[n_y + 5:]
    if route:
        rw_ref, rb_ref, xo_ref, hp_ref, rt_ref = rest
    else:
        xo_ref, h_ref = rest
    tm = x_ref.shape[1]
    rows = min(tm, OUTPROJ_ROW_GROUP)
    for c in range(tm // rows):
        rs = slice(c * rows, (c + 1) * rows)
        y = y_refs[0][0, rs, :]
        if n_y == 2:
            y = (y.astype(F32) + y_refs[1][0, rs, :].astype(F32)).astype(BF16)
        o = jnp.dot(y, w_ref[...], preferred_element_type=F32)
        xn = x_ref[0, rs, :] + mod_ref[0, G1:G1 + 1, :] * _rms_rows(o, gpost_ref[...])
        xo_ref[0, rs, :] = xn
        h = _rms_rows(xn, gpre_ref[...]) * (1.0 + mod_ref[0, SC2:SC2 + 1, :]) + mod_ref[0, SH2:SH2 + 1, :]
        if not route:
            h_ref[0, rs, :] = h.astype(h_ref.dtype)
            continue
        hp_ref[0, rs, :] = _pack_rows(h)
        logits = jnp.dot(h, rw_ref[...], preferred_element_type=F32,
                         precision=lax.Precision.HIGHEST) + rb_ref[...]
        lane = lax.broadcasted_iota(I32, logits.shape, 1).astype(F32)
        m1 = jnp.max(logits, axis=-1, keepdims=True)
        i1 = jnp.min(jnp.where(logits == m1, lane, float(LANES)), axis=-1, keepdims=True)
        masked = jnp.where(lane == i1, NEG_BIG, logits)
        m2 = jnp.max(masked, axis=-1, keepdims=True)
        i2 = jnp.min(jnp.where(masked == m2, lane, float(LANES)), axis=-1, keepdims=True)
        e2 = jnp.exp(m2 - m1)
        den = 1.0 + e2
        w1 = 1.0 / den
        w2 = e2 / den
        rt_ref[0, rs, :] = jnp.where(lane == 0.0, i1,
                                     jnp.where(lane == 1.0, i2,
                                               jnp.where(lane == 2.0, w1, jnp.where(lane == 3.0, w2, 0.0))))


def _outproj(ys, w, x, gpost, gpre, mod, mod_row, tm, router=None):
    b, t, d = x.shape
    kdim = w.shape[0]
    route = router is not None
    in_specs = [pl.BlockSpec((1, tm, kdim), lambda bb, i: (bb, i, 0)) for _ in ys]
    in_specs += [pl.BlockSpec((kdim, d), lambda bb, i: (0, 0)),
                 pl.BlockSpec((1, tm, d), lambda bb, i: (bb, i, 0)),
                 pl.BlockSpec((1, d), lambda bb, i: (0, 0)),
                 pl.BlockSpec((1, d), lambda bb, i: (0, 0)),
                 pl.BlockSpec((1, 6, d), lambda bb, i: (mod_row(bb), 0, 0))]
    args = list(ys) + [w, x, gpost.reshape(1, d), gpre.reshape(1, d), mod]
    out_specs = [pl.BlockSpec((1, tm, d), lambda bb, i: (bb, i, 0))]
    out_shape = [jax.ShapeDtypeStruct((b, t, d), F32)]
    if route:
        rw, rb = router
        in_specs += [pl.BlockSpec((d, LANES), lambda bb, i: (0, 0)),
                     pl.BlockSpec((1, LANES), lambda bb, i: (0, 0))]
        args += [rw, rb]
        out_specs += [pl.BlockSpec((1, tm, d // 2), lambda bb, i: (bb, i, 0)),
                      pl.BlockSpec((1, tm, LANES), lambda bb, i: (bb, i, 0))]
        out_shape += [jax.ShapeDtypeStruct((b, t, d // 2), U32),
                      jax.ShapeDtypeStruct((b, t, LANES), F32)]
    else:
        out_specs.append(pl.BlockSpec((1, tm, d), lambda bb, i: (bb, i, 0)))
        out_shape.append(jax.ShapeDtypeStruct((b, t, d), BF16))
    return pl.pallas_call(
        functools.partial(_outproj_kernel, n_y=len(ys), route=route),
        grid=(b, t // tm),
        in_specs=in_specs,
        out_specs=out_specs,
        out_shape=out_shape,
        compiler_params=_cparams(("arbitrary", "arbitrary")),
        name="outproj",
    )(*args)


def _ffn_kernel(h_ref, wg_ref, wu_ref, wd_ref, x_ref, gpost_ref, mod_ref, o_ref, acc):
    f = pl.program_id(2)
    h = h_ref[0]
    g = jnp.dot(h, wg_ref[...], preferred_element_type=F32)
    u = jnp.dot(h, wu_ref[...], preferred_element_type=F32)
    a = (_silu(g) * u).astype(BF16)
    part = jnp.dot(a, wd_ref[...], preferred_element_type=F32)

    @pl.when(f == 0)
    def _():
        acc[...] = part

    @pl.when(f > 0)
    def _():
        acc[...] += part

    @pl.when(f == pl.num_programs(2) - 1)
    def _():
        o_ref[0] = x_ref[0] + mod_ref[0, G2:G2 + 1, :] * _rms_rows(acc[...], gpost_ref[...])


def _ffn(h, w_gu, w_down, x, gpost, mod, mod_row, tm, nf):
    b, t, d = x.shape
    dff = w_down.shape[0]
    tf = dff // nf
    return pl.pallas_call(
        _ffn_kernel,
        grid=(b, t // tm, nf),
        in_specs=[pl.BlockSpec((1, tm, d), lambda bb, i, f: (bb, i, 0)),
                  pl.BlockSpec((d, tf), lambda bb, i, f: (0, f)),
                  pl.BlockSpec((d, tf), lambda bb, i, f: (0, f + nf)),
                  pl.BlockSpec((tf, d), lambda bb, i, f: (f, 0)),
                  pl.BlockSpec((1, tm, d), lambda bb, i, f: (bb, i, 0)),
                  pl.BlockSpec((1, d), lambda bb, i, f: (0, 0)),
                  pl.BlockSpec((1, 6, d), lambda bb, i, f: (mod_row(bb), 0, 0))],
        out_specs=pl.BlockSpec((1, tm, d), lambda bb, i, f: (bb, i, 0)),
        out_shape=jax.ShapeDtypeStruct((b, t, d), F32),
        scratch_shapes=[pltpu.VMEM((tm, d), F32)],
        compiler_params=_cparams(("arbitrary", "arbitrary", "arbitrary")),
        name="ffn",
    )(h, w_gu, w_gu, w_down, x, gpost.reshape(1, d), mod)


def _attn_kernel(q_ref, kl_ref, vl_ref, kc_ref, vc_ref, lam_ref, subln_ref, o_ref, ks, vs, sa, sb, ma, mb,
                 *, lambda_init, nq):
    i = pl.program_id(2)
    t = kl_ref.shape[1]

    @pl.when(i == 0)
    def _():
        ks[0:t, :] = kl_ref[0]
        ks[t:, :] = kc_ref[0]
        vs[0:t, 0:LANES] = vl_ref[0]
        vs[t:, 0:LANES] = vc_ref[0]
        vs[:, LANES:] = jnp.ones((vs.shape[0], LANES), BF16)

    def scores(s_scr, m_scr):
        q = q_ref[0]
        k = ks[...]
        lane = lax.broadcasted_iota(I32, q.shape, 1)
        zero = jnp.zeros_like(q)
        for h, qh in enumerate((jnp.where(lane < DIFF_HEAD_DIM, q, zero),
                                jnp.where(lane >= DIFF_HEAD_DIM, q, zero))):
            s = lax.dot_general(qh, k, (((1,), (1,)), ((), ())), preferred_element_type=F32)
            s_scr[h] = s
            m_scr[h] = jnp.max(s, axis=-1, keepdims=True)

    def finish(s_scr, m_scr):
        v = vs[...]
        lp = lam_ref[...]
        lam = (jnp.exp(jnp.sum(lp[0:1] * lp[1:2], axis=-1, keepdims=True))
               - jnp.exp(jnp.sum(lp[2:3] * lp[3:4], axis=-1, keepdims=True)) + lambda_init)
        outs = []
        for h in range(2):
            p = jnp.exp2(s_scr[h] - m_scr[h]).astype(BF16)
            r = jnp.dot(p, v, preferred_element_type=F32)
            outs.append(r[:, 0:LANES] / r[:, LANES:LANES + 1])
        o = outs[0] - lam * outs[1]
        o_ref[0] = (_rms_rows(o, subln_ref[...]) * (1.0 - lambda_init)).astype(o_ref.dtype)

    middle = jnp.logical_and(i > 0, i < nq)
    even = i % 2 == 0

    @pl.when(i == 0)
    def _():
        scores(sa, ma)

    @pl.when(jnp.logical_and(middle, even))
    def _():
        scores(sa, ma)
        finish(sb, mb)

    @pl.when(jnp.logical_and(middle, jnp.logical_not(even)))
    def _():
        scores(sb, mb)
        finish(sa, ma)

    @pl.when(i == nq)
    def _():
        if (nq - 1) % 2 == 0:
            finish(sa, ma)
        else:
            finish(sb, mb)


def _attention(z, zc, lam_params, subln, tq, lambda_init):
    b, t, n3 = z.shape
    tc = zc.shape[1]
    d = n3 // 3
    npair = d // LANES
    nq = t // tq
    s_scr = pltpu.VMEM((2, tq, t + tc), F32)
    m_scr = pltpu.VMEM((2, tq, 1), F32)
    return pl.pallas_call(
        functools.partial(_attn_kernel, lambda_init=lambda_init, nq=nq),
        grid=(b, npair, nq + 1),
        in_specs=[pl.BlockSpec((1, tq, LANES), lambda bb, p, i: (bb, jnp.minimum(i, nq - 1), p)),
                  pl.BlockSpec((1, t, LANES), lambda bb, p, i: (bb, 0, npair + p)),
                  pl.BlockSpec((1, t, LANES), lambda bb, p, i: (bb, 0, 2 * npair + p)),
                  pl.BlockSpec((1, tc, LANES), lambda bb, p, i: (bb, 0, npair + p)),
                  pl.BlockSpec((1, tc, LANES), lambda bb, p, i: (bb, 0, 2 * npair + p)),
                  pl.BlockSpec(lam_params.shape, lambda bb, p, i: (0, 0)),
                  pl.BlockSpec((1, LANES), lambda bb, p, i: (0, 0))],
        out_specs=pl.BlockSpec((1, tq, LANES), lambda bb, p, i: (bb, jnp.maximum(i - 1, 0), p)),
        out_shape=jax.ShapeDtypeStruct((b, t, d), BF16),
        scratch_shapes=[pltpu.VMEM((t + tc, LANES), BF16), pltpu.VMEM((t + tc, 2 * LANES), BF16),
                        s_scr, s_scr, m_scr, m_scr],
        compiler_params=_cparams(("arbitrary", "arbitrary", "arbitrary")),
        name="attention",
    )(z, z, z, zc, zc, lam_params.astype(F32), subln.reshape(1, LANES).astype(F32))


def _moe_kernel(te_ref, tv_ref, pinv_ref, inv_ref, ninv_ref, wg_ref, wu_ref, wd_ref, h_hbm,
                y_hbm, gbuf, hbuf, acc, obuf, gsem, ssem, *, tm, n_tok):
    del te_ref
    j = pl.program_id(0)
    f = pl.program_id(1)
    nf = pl.num_programs(1)
    slot = j % 2
    other = 1 - slot
    rows_per_step = tm // MOE_HIDDEN_STEPS
    valid = tv_ref[j] > 0
    drain = jnp.logical_and(jnp.logical_not(valid), tv_ref[jnp.maximum(j - 1, 0)] * (j > 0).astype(I32) > 0)

    def gather_row(idx_ref, to_slot, r):
        tok = jnp.minimum(idx_ref[0, 0, r] >> 1, n_tok - 1)
        pltpu.make_async_copy(h_hbm.at[pl.ds(tok, 1)], gbuf.at[to_slot, pl.ds(r, 1)], gsem.at[to_slot]).start()

    def scatter_row(idx_ref, from_slot, r):
        a = idx_ref[0, 0, r]
        row = jnp.where(a < 2 * n_tok, (a & 1) * n_tok + (a >> 1), 2 * n_tok + r)
        pltpu.make_async_copy(obuf.at[from_slot, pl.ds(r, 1)], y_hbm.at[pl.ds(row, 1)],
                              ssem.at[from_slot]).start()

    def loop_rows(fn):
        def body(r, c):
            fn(r)
            return c
        lax.fori_loop(0, tm, body, 0, unroll=DMA_UNROLL)

    def wait_gather(s):
        pltpu.make_async_copy(h_hbm.at[pl.ds(0, tm)], gbuf.at[s], gsem.at[s]).wait()

    def wait_scatter(s):
        pltpu.make_async_copy(obuf.at[s], y_hbm.at[pl.ds(0, tm)], ssem.at[s]).wait()

    @pl.when(jnp.logical_and(j == 0, f == 0))
    def _():
        obuf[1] = jnp.zeros(obuf.shape[1:], U32)
        loop_rows(lambda r: gather_row(inv_ref, 0, r))

    @pl.when(jnp.logical_and(jnp.logical_or(valid, drain), f == 0))
    def _():
        wait_gather(slot)

    @pl.when(jnp.logical_and(valid, f == 0))
    def _():
        lo, hi = _unpack_rows(gbuf[slot])
        half = lo.shape[1]
        hbuf[:, :half] = lo.astype(BF16)
        hbuf[:, half:] = hi.astype(BF16)

    @pl.when(valid)
    def _():
        h = hbuf[...]
        g = jnp.dot(h, wg_ref[0], preferred_element_type=F32)
        u = jnp.dot(h, wu_ref[0], preferred_element_type=F32)
        a = (_silu(g) * u).astype(BF16)
        part = jnp.dot(a, wd_ref[0], preferred_element_type=F32)
        base = f * rows_per_step
        for k in range(rows_per_step):
            gather_row(ninv_ref, other, base + k)
            scatter_row(pinv_ref, other, base + k)

        @pl.when(f == 0)
        def _():
            acc[...] = part

        @pl.when(f > 0)
        def _():
            acc[...] += part

    @pl.when(jnp.logical_and(valid, f == nf - 1))
    def _():
        @pl.when(j > 0)
        def _():
            wait_scatter(slot)

        obuf[slot] = _pack_rows(acc[...])

    @pl.when(jnp.logical_and(drain, f == 0))
    def _():
        loop_rows(lambda r: scatter_row(pinv_ref, other, r))
        wait_scatter(other)
        wait_scatter(slot)


def _moe(hp, te, tv, inv, w_gu, w_down, tm):
    nt = te.shape[0]
    n_tok, half = hp.shape
    e, d, two_ff = w_gu.shape
    dff = two_ff // 2
    nf = MOE_HIDDEN_STEPS
    tf = dff // nf

    def wf(j, f, te_r, tv_r):
        return jnp.where(tv_r[j] > 0, f, nf - 1)

    def inv_spec(off):
        return pl.BlockSpec((1, 1, tm), lambda j, f, a, c: (jnp.minimum(j + off, nt), 0, 0),
                            memory_space=pltpu.SMEM)

    grid_spec = pltpu.PrefetchScalarGridSpec(
        num_scalar_prefetch=2,
        grid=(nt, nf),
        in_specs=[inv_spec(0), inv_spec(1), inv_spec(2),
                  pl.BlockSpec((1, d, tf), lambda j, f, a, c: (a[j], 0, wf(j, f, a, c))),
                  pl.BlockSpec((1, d, tf), lambda j, f, a, c: (a[j], 0, wf(j, f, a, c) + nf)),
                  pl.BlockSpec((1, tf, d), lambda j, f, a, c: (a[j], wf(j, f, a, c), 0)),
                  pl.BlockSpec(memory_space=pl.ANY)],
        out_specs=pl.BlockSpec(memory_space=pl.ANY),
        scratch_shapes=[pltpu.VMEM((2, tm, half), U32),
                        pltpu.VMEM((tm, d), BF16),
                        pltpu.VMEM((tm, d), F32),
                        pltpu.VMEM((2, tm, half), U32),
                        pltpu.SemaphoreType.DMA((2,)),
                        pltpu.SemaphoreType.DMA((2,))],
    )
    return pl.pallas_call(
        functools.partial(_moe_kernel, tm=tm, n_tok=n_tok),
        grid_spec=grid_spec,
        out_shape=jax.ShapeDtypeStruct((2 * n_tok + tm, half), U32),
        compiler_params=_cparams(("arbitrary", "arbitrary")),
        name="moe",
    )(te, tv, inv, inv, inv, w_gu, w_gu, w_down, hp)


def _route_plan(rt, tm, nt):
    t = rt.shape[0]
    e_flat = rt[:, 0:2].astype(I32).reshape(-1)
    onehot = (e_flat[:, None] == jnp.arange(N_EXPERTS, dtype=I32)[None, :]).astype(I32)
    csum = jnp.cumsum(onehot, axis=0)
    rank = jnp.sum(csum * onehot, axis=1) - 1
    counts = csum[-1]
    tiles_e = (counts + tm - 1) // tm
    tile_end = jnp.cumsum(tiles_e)
    row_off = (tile_end - tiles_e) * tm
    dest = jnp.sum(row_off[None, :] * onehot, axis=1) + rank
    inv = jnp.full((nt * tm,), 2 * t, I32).at[dest].set(jnp.arange(2 * t, dtype=I32))
    n_used = tile_end[-1]
    tid = jnp.arange(nt, dtype=I32)
    tv = (tid < n_used).astype(I32)
    te = jnp.sum((jnp.minimum(tid, n_used - 1)[:, None] >= tile_end[None, :]).astype(I32), axis=1)
    inv = jnp.concatenate([jnp.full((tm,), 2 * t, I32), inv])
    return jnp.minimum(te, N_EXPERTS - 1), tv, inv.reshape(nt + 1, 1, tm)


def _combine_kernel(y0_ref, y1_ref, rt_ref, x_ref, gpost_ref, mod_ref, o_ref):
    lo0, hi0 = _unpack_rows(y0_ref[...])
    lo1, hi1 = _unpack_rows(y1_ref[...])
    w0 = rt_ref[0, :, 2:3]
    w1 = rt_ref[0, :, 3:4]
    y = jnp.concatenate([w0 * lo0 + w1 * lo1, w0 * hi0 + w1 * hi1], axis=1)
    o_ref[0] = x_ref[0] + mod_ref[0, G2:G2 + 1, :] * _rms_rows(y, gpost_ref[...])


def _combine(y2, rt, x, gpost, mod, tm):
    b, t, d = x.shape
    per_b = t // tm
    nblk = b * per_b
    return pl.pallas_call(
        _combine_kernel,
        grid=(b, per_b),
        in_specs=[pl.BlockSpec((tm, d // 2), lambda bb, i: (bb * per_b + i, 0)),
                  pl.BlockSpec((tm, d // 2), lambda bb, i: (nblk + bb * per_b + i, 0)),
                  pl.BlockSpec((1, tm, LANES), lambda bb, i: (bb, i, 0)),
                  pl.BlockSpec((1, tm, d), lambda bb, i: (bb, i, 0)),
                  pl.BlockSpec((1, d), lambda bb, i: (0, 0)),
                  pl.BlockSpec((1, 6, d), lambda bb, i: (bb, 0, 0))],
        out_specs=pl.BlockSpec((1, tm, d), lambda bb, i: (bb, i, 0)),
        out_shape=jax.ShapeDtypeStruct((b, t, d), F32),
        compiler_params=_cparams(("arbitrary", "arbitrary")),
        name="combine",
    )(y2, y2, rt, x, gpost.reshape(1, d), mod)


def _rope_angles(rows, head_dim):
    pos_r = jnp.repeat(jnp.arange(rows, dtype=F32), GRID_W)
    pos_c = jnp.tile(jnp.arange(GRID_W, dtype=F32), rows)
    axis_dim = head_dim // 2
    freqs = ROPE_BASE ** (-jnp.arange(0, axis_dim, 2, dtype=F32) / axis_dim)
    return pos_r[:, None] * freqs[None, :], pos_c[:, None] * freqs[None, :]


def _rope_tables(rows, head_dim, reps):
    ar, ac = _rope_angles(rows, head_dim)
    cos = jnp.concatenate([jnp.cos(ar), jnp.cos(ar), jnp.cos(ac), jnp.cos(ac)], axis=1)
    zr = jnp.zeros_like(ar)
    s_up = jnp.concatenate([-jnp.sin(ar), zr, -jnp.sin(ac), zr], axis=1)
    s_dn = jnp.concatenate([zr, jnp.sin(ar), zr, jnp.sin(ac)], axis=1)
    return jnp.stack([jnp.tile(cos, (1, reps)), jnp.tile(s_up, (1, reps)), jnp.tile(s_dn, (1, reps))])


def _identity_tables(n_tab, tm, period):
    return jnp.concatenate([jnp.ones((1, tm, period), F32), jnp.zeros((n_tab - 1, tm, period), F32)])


def _tile(t, want):
    return min(t, want)


def kernel(x, c, ctx, c_ctx, l0_w_mod, l0_b_mod, l0_norm_pre_mix, l0_norm_post_mix, l0_norm_pre_ffn, l0_norm_post_ffn, l0_ret_w_in, l0_ret_decay_logit, l0_ret_w_out, l0_ffn_w_gate_up, l0_ffn_w_down, l1_w_mod, l1_b_mod, l1_norm_pre_mix, l1_norm_post_mix, l1_norm_pre_ffn, l1_norm_post_ffn, l1_attn_w_in, l1_attn_lambda, l1_attn_subln, l1_attn_w_out, l1_router_w, l1_router_b, l1_moe_w_gate_up, l1_moe_w_down):
    b, t, d = x.shape
    tc = ctx.shape[1]
    rows = t // GRID_W
    assert t % tc == 0

    n_rows = -(-(b + 1) // 8) * 8
    cc = jnp.concatenate([c, c_ctx[None, :], jnp.zeros((n_rows - b - 1, d), F32)], axis=0)
    lat_row = lambda bb: bb
    ctx_row = lambda bb: b

    tm_lat = _tile(t, 1024)
    tn = 1024

    mod0 = _modulation(cc, l0_w_mod, l0_b_mod)
    w_in0 = l0_ret_w_in.astype(BF16)
    dk = w_in0.shape[1] // (8 * RET_HEADS)
    tab_ret = _rope_tables(rows, dk, 1)
    tab_ret = jnp.stack([tab_ret[0], tab_ret[1] + tab_ret[2]])
    ret_scales = (1.0,) * (RET_HEADS * dk // tn) + (dk ** -0.5,) * (RET_HEADS * dk // tn)
    z0 = _inproj(x, l0_norm_pre_mix, mod0, lat_row, w_in0, tab_ret, True, tm_lat, tn, ret_scales, (dk // 4,))
    zc0 = _inproj(ctx, l0_norm_pre_mix, mod0, ctx_row, w_in0, _identity_tables(2, tc, dk), False, tc, tn,
                  ret_scales, (dk // 4,))
    yf, yb, yfc, ybc = _retention(z0, zc0, l0_ret_decay_logit)

    w_out0 = l0_ret_w_out.astype(BF16)
    tm_o = _tile(t, 512)
    x1, h1 = _outproj([yf, yb], w_out0, x, l0_norm_post_mix, l0_norm_pre_ffn, mod0, lat_row, tm_o)
    c1, hc1 = _outproj([yfc, ybc], w_out0, ctx, l0_norm_post_mix, l0_norm_pre_ffn, mod0, ctx_row, tc)

    w_gu0 = l0_ffn_w_gate_up.astype(BF16)
    w_dn0 = l0_ffn_w_down.astype(BF16)
    x2 = _ffn(h1, w_gu0, w_dn0, x1, l0_norm_post_ffn, mod0, lat_row, tm_lat, 2)
    c2 = _ffn(hc1, w_gu0, w_dn0, c1, l0_norm_post_ffn, mod0, ctx_row, tc, 2)

    lambda_init = 0.8 - 0.6 * math.exp(-0.3 * 1)
    mod1 = _modulation(cc, l1_w_mod, l1_b_mod)
    w_in1 = l1_attn_w_in.astype(BF16)
    tab_diff = _rope_tables(rows, DIFF_HEAD_DIM, LANES // DIFF_HEAD_DIM)
    q4 = DIFF_HEAD_DIM // 4
    diff_scales = (DIFF_HEAD_DIM ** -0.5 * LOG2_E,) * (d // tn) + (1.0,) * (d // tn)
    z1 = _inproj(x2, l1_norm_pre_mix, mod1, lat_row, w_in1, tab_diff, True, tm_lat, tn, diff_scales,
                 (LANES - q4, q4))
    zc1 = _inproj(c2, l1_norm_pre_mix, mod1, ctx_row, w_in1, _identity_tables(3, tc, LANES), False, tc, tn,
                  (1.0,) * (2 * d // tn), (LANES - q4, q4))
    o1 = _attention(z1, zc1, l1_attn_lambda, l1_attn_subln, _tile(t, 256), lambda_init)

    rw = jnp.zeros((d, LANES), F32).at[:, :N_EXPERTS].set(l1_router_w)
    rb = jnp.full((1, LANES), NEG_BIG, F32).at[0, :N_EXPERTS].set(l1_router_b)
    x3, hp, rt = _outproj([o1], l1_attn_w_out.astype(BF16), x2, l1_norm_post_mix, l1_norm_pre_ffn, mod1,
                          lat_row, tm_o, router=(rw, rb))

    n_tok = b * t
    tm_e = _tile(n_tok, 1024)
    nt = 2 * n_tok // tm_e + N_EXPERTS
    te, tv, inv = _route_plan(rt.reshape(n_tok, LANES), tm_e, nt)
    y2 = _moe(hp.reshape(n_tok, d // 2), te, tv, inv, l1_moe_w_gate_up.astype(BF16),
              l1_moe_w_down.astype(BF16), tm_e)
    return _combine(y2, rt, x3, l1_norm_post_ffn, mod1, tm_o)
```

```python
import functools
import math

import jax
import jax.numpy as jnp
from jax import lax
from jax.experimental import pallas as pl
from jax.experimental.pallas import tpu as pltpu

F32 = jnp.float32
BF16 = jnp.bfloat16
U32 = jnp.uint32
I32 = jnp.int32

EPS = 1e-6
GRID_W = 64
ROPE_BASE = 10000.0
RET_HEADS = 4
DIFF_HEADS = 8
DIFF_HEAD_DIM = 64
N_EXPERTS = 8
LANES = 128
SUBLANES = 8
NEG_BIG = -1e30
VMEM_LIMIT = 56 * 1024 * 1024
DMA_UNROLL = 8
MOE_HIDDEN_STEPS = 4
FFN_HIDDEN_TILE = 256
RET_CHUNKS_PER_STEP = 2
OUTPROJ_ROW_GROUP = 256
LOG2_E = 1.4426950408889634

SH1, SC1, G1, SH2, SC2, G2 = range(6)


def _cparams(sem):
    return pltpu.CompilerParams(dimension_semantics=sem, vmem_limit_bytes=VMEM_LIMIT)


def _rms_rows(x, gain):
    return x * lax.rsqrt(jnp.mean(x * x, axis=-1, keepdims=True) + EPS) * gain


def _silu(x):
    return x * jax.nn.sigmoid(x)


def _mod_kernel(c_ref, w_ref, b_ref, o_ref):
    a = _silu(c_ref[...])
    o_ref[...] = jnp.dot(a, w_ref[...], preferred_element_type=F32,
                         precision=lax.Precision.HIGHEST) + b_ref[...]


def _modulation(cc, w_mod, b_mod):
    r, d = cc.shape
    n = w_mod.shape[1]
    tn = n // 6
    out = pl.pallas_call(
        _mod_kernel,
        grid=(n // tn,),
        in_specs=[pl.BlockSpec((r, d), lambda j: (0, 0)),
                  pl.BlockSpec((d, tn), lambda j: (0, j)),
                  pl.BlockSpec((1, tn), lambda j: (0, j))],
        out_specs=pl.BlockSpec((r, tn), lambda j: (0, j)),
        out_shape=jax.ShapeDtypeStruct((r, n), F32),
        compiler_params=_cparams(("arbitrary",)),
        name="mod",
    )(cc, w_mod, b_mod.reshape(1, n))
    return out.reshape(r, 6, d)


def _inproj_kernel(x_ref, g_ref, mod_ref, w_ref, tab_ref, o_ref, *, tn, rope_scales, shifts, period):
    y = _rms_rows(x_ref[0], g_ref[...])
    h = (y * (1.0 + mod_ref[0, SC1:SC1 + 1, :]) + mod_ref[0, SH1:SH1 + 1, :]).astype(BF16)
    for j in range(w_ref.shape[1] // tn):
        acc = jnp.dot(h, w_ref[:, j * tn:(j + 1) * tn], preferred_element_type=F32)
        if j >= len(rope_scales):
            o_ref[0, :, j * tn:(j + 1) * tn] = acc.astype(o_ref.dtype)
            continue
        for k in range(tn // LANES):
            xs = acc[:, k * LANES:(k + 1) * LANES]
            p = (k * LANES) % period
            r = xs * tab_ref[0, :, p:p + LANES]
            for si, sft in enumerate(shifts):
                r = r + pltpu.roll(xs, sft, 1) * tab_ref[1 + si, :, p:p + LANES]
            if rope_scales[j] != 1.0:
                r = r * rope_scales[j]
            c0 = j * tn + k * LANES
            o_ref[0, :, c0:c0 + LANES] = r.astype(o_ref.dtype)


def _inproj(x, gain, mod, mod_row, w, tab, tab_per_tile, tm, tn, rope_scales, shifts):
    b, t, d = x.shape
    n = w.shape[1]
    period = tab.shape[2]
    kern = functools.partial(_inproj_kernel, tn=tn, rope_scales=rope_scales, shifts=shifts, period=period)
    tab_map = (lambda bb, i: (0, i, 0)) if tab_per_tile else (lambda bb, i: (0, 0, 0))
    return pl.pallas_call(
        kern,
        grid=(b, t // tm),
        in_specs=[pl.BlockSpec((1, tm, d), lambda bb, i: (bb, i, 0)),
                  pl.BlockSpec((1, d), lambda bb, i: (0, 0)),
                  pl.BlockSpec((1, 6, d), lambda bb, i: (mod_row(bb), 0, 0)),
                  pl.BlockSpec((d, n), lambda bb, i: (0, 0), pipeline_mode=pl.Buffered(1)),
                  pl.BlockSpec((tab.shape[0], tm, period), tab_map)],
        out_specs=pl.BlockSpec((1, tm, n), lambda bb, i: (bb, i, 0)),
        out_shape=jax.ShapeDtypeStruct((b, t, n), BF16),
        compiler_params=_cparams(("arbitrary", "arbitrary")),
        name="inproj",
    )(x, gain.reshape(1, d), mod, w, tab)


def _ret_kernel(qf, kf, vf, gf, qb, kb, vb, gb, qc, kc, vc, gfc, gbc, intra, qdec, kdec, cdec,
                yf, yb, yfc, ybc, st):
    s = pl.program_id(2)
    chunk = qc.shape[1]

    def direction(d, q_ref, k_ref, v_ref, g_ref, y_ref, sub):
        rs = slice(sub * chunk, (sub + 1) * chunk)
        q = q_ref[0, rs, :]
        k = k_ref[0, rs, :]
        v = v_ref[0, rs, :]
        scores = lax.dot_general(q, k, (((1,), (1,)), ((), ())), preferred_element_type=F32) * intra[d, 0]
        state = st[d]
        o = (jnp.dot(scores.astype(BF16), v, preferred_element_type=F32)
             + qdec[d, 0] * jnp.dot(q, state.astype(BF16), preferred_element_type=F32))
        ks = (k.astype(F32) * kdec[d, 0]).astype(BF16)
        st[d] = state * cdec[d, 0] + lax.dot_general(ks, v, (((0,), (0,)), ((), ())),
                                                     preferred_element_type=F32)
        hn = o * lax.rsqrt(jnp.mean(o * o, axis=-1, keepdims=True) + EPS)
        y_ref[0, rs, :] = (_silu(g_ref[0, rs, :].astype(F32)) * hn).astype(y_ref.dtype)

    @pl.when(s == 0)
    def _():
        st[...] = jnp.zeros(st.shape, F32)
        direction(0, qc, kc, vc, gfc, yfc, 0)
        direction(1, qc, kc, vc, gbc, ybc, 0)

    @pl.when(s > 0)
    def _():
        for c in range(RET_CHUNKS_PER_STEP):
            direction(0, qf, kf, vf, gf, yf, c)
            direction(1, qb, kb, vb, gb, yb, RET_CHUNKS_PER_STEP - 1 - c)


def _retention(z, zc, decay_logit):
    b, t, _ = z.shape
    chunk = zc.shape[1]
    n_lat = t // chunk
    h = RET_HEADS
    dk = z.shape[2] // (8 * h)
    dv = 2 * dk

    log_g = jax.nn.log_sigmoid(decay_logit.astype(F32))
    idx = jnp.arange(chunk, dtype=F32)
    diff = idx[:, None] - idx[None, :]
    lg = log_g[:, :, None, None]
    intra_f = jnp.where(diff >= 0, jnp.exp(lg[0] * jnp.maximum(diff, 0.0)), 0.0)
    intra_b = jnp.where(diff <= 0, jnp.exp(lg[1] * jnp.maximum(-diff, 0.0)), 0.0)
    intra = jnp.stack([intra_f, intra_b])
    qdec = jnp.stack([jnp.exp(log_g[0][:, None] * (idx + 1.0)),
                      jnp.exp(log_g[1][:, None] * (chunk - idx))])[..., None]
    kdec = jnp.stack([jnp.exp(log_g[0][:, None] * (chunk - 1.0 - idx)),
                      jnp.exp(log_g[1][:, None] * idx)])[..., None]
    cdec = jnp.exp(log_g * chunk)[..., None, None]

    cps = RET_CHUNKS_PER_STEP
    assert n_lat % cps == 0
    n_blk = n_lat // cps

    def fwd(s):
        return jnp.maximum(s - 1, 0)

    def bwd(s):
        return n_blk - jnp.maximum(s, 1)

    def spec(width, col0, chunk_of):
        return pl.BlockSpec((1, cps * chunk, width), lambda bb, hh, s: (bb, chunk_of(s), col0 + hh))

    def cspec(width, col0):
        return pl.BlockSpec((1, chunk, width), lambda bb, hh, s: (bb, 0, col0 + hh))

    in_specs = [spec(dk, 0, fwd), spec(dk, h, fwd), spec(dv, h, fwd), spec(dv, 2 * h, fwd),
                spec(dk, 0, bwd), spec(dk, h, bwd), spec(dv, h, bwd), spec(dv, 3 * h, bwd),
                cspec(dk, 0), cspec(dk, h), cspec(dv, h), cspec(dv, 2 * h), cspec(dv, 3 * h),
                pl.BlockSpec((2, 1, chunk, chunk), lambda bb, hh, s: (0, hh, 0, 0)),
                pl.BlockSpec((2, 1, chunk, 1), lambda bb, hh, s: (0, hh, 0, 0)),
                pl.BlockSpec((2, 1, chunk, 1), lambda bb, hh, s: (0, hh, 0, 0)),
                pl.BlockSpec((2, 1, 1, 1), lambda bb, hh, s: (0, hh, 0, 0))]
    out_specs = [pl.BlockSpec((1, cps * chunk, dv), lambda bb, hh, s: (bb, fwd(s), hh)),
                 pl.BlockSpec((1, cps * chunk, dv), lambda bb, hh, s: (bb, bwd(s), hh)),
                 pl.BlockSpec((1, chunk, dv), lambda bb, hh, s: (bb, 0, hh)),
                 pl.BlockSpec((1, chunk, dv), lambda bb, hh, s: (bb, 0, hh))]
    yl = jax.ShapeDtypeStruct((b, t, h * dv), BF16)
    yc = jax.ShapeDtypeStruct((b, chunk, h * dv), BF16)
    return pl.pallas_call(
        _ret_kernel,
        grid=(b, h, n_blk + 1),
        in_specs=in_specs,
        out_specs=out_specs,
        out_shape=[yl, yl, yc, yc],
        scratch_shapes=[pltpu.VMEM((2, dk, dv), F32)],
        compiler_params=_cparams(("arbitrary", "arbitrary", "arbitrary")),
        name="retention",
    )(z, z, z, z, z, z, z, z, zc, zc, zc, zc, zc, intra, qdec, kdec, cdec)


def _pack_rows(h):
    n = h.shape[1] // 2
    bits = lax.bitcast_convert_type(h.astype(BF16).astype(F32), U32)
    return (bits[:, :n] >> 16) | (bits[:, n:] & jnp.uint32(0xFFFF0000))


def _unpack_rows(w):
    lo = lax.bitcast_convert_type(w << 16, F32)
    hi = lax.bitcast_convert_type(w & jnp.uint32(0xFFFF0000), F32)
    return lo, hi


def _outproj_kernel(*refs, n_y, route, y_per_group):
    y_refs = refs[:n_y]
    w_ref, x_ref, gpost_ref, gpre_ref, mod_ref = refs[n_y:n_y + 5]
    rest = refs[n_y + 5:]
    if route:
        rw_ref, rb_ref, xo_ref, hp_ref, rt_ref = rest
    else:
        xo_ref, h_ref = rest
    tm = x_ref.shape[1]
    rows = min(tm, OUTPROJ_ROW_GROUP)
    for c in range(tm // rows):
        rs = slice(c * rows, (c + 1) * rows)
        if y_per_group:
            y = y_refs[c][0]
        else:
            y = y_refs[0][0, rs, :]
            if n_y == 2:
                y = (y.astype(F32) + y_refs[1][0, rs, :].astype(F32)).astype(BF16)
        o = jnp.dot(y, w_ref[...], preferred_element_type=F32)
        xn = x_ref[0, rs, :] + mod_ref[0, G1:G1 + 1, :] * _rms_rows(o, gpost_ref[...])
        xo_ref[0, rs, :] = xn
        h = _rms_rows(xn, gpre_ref[...]) * (1.0 + mod_ref[0, SC2:SC2 + 1, :]) + mod_ref[0, SH2:SH2 + 1, :]
        if not route:
            h_ref[0, rs, :] = h.astype(h_ref.dtype)
            continue
        packed = _pack_rows(h)
        for sl in range(packed.shape[1] // LANES):
            hp_ref[0, rs, sl, 0, :] = packed[:, sl * LANES:(sl + 1) * LANES]
        logits = jnp.dot(h, rw_ref[...], preferred_element_type=F32,
                         precision=lax.Precision.HIGHEST) + rb_ref[...]
        lane = lax.broadcasted_iota(I32, logits.shape, 1).astype(F32)
        m1 = jnp.max(logits, axis=-1, keepdims=True)
        i1 = jnp.min(jnp.where(logits == m1, lane, float(LANES)), axis=-1, keepdims=True)
        masked = jnp.where(lane == i1, NEG_BIG, logits)
        m2 = jnp.max(masked, axis=-1, keepdims=True)
        i2 = jnp.min(jnp.where(masked == m2, lane, float(LANES)), axis=-1, keepdims=True)
        e2 = jnp.exp(m2 - m1)
        den = 1.0 + e2
        w1 = 1.0 / den
        w2 = e2 / den
        rt_ref[0, rs, :] = jnp.where(lane == 0.0, i1,
                                     jnp.where(lane == 1.0, i2,
                                               jnp.where(lane == 2.0, w1, jnp.where(lane == 3.0, w2, 0.0))))


def _outproj(ys, w, x, gpost, gpre, mod, mod_row, tm, router=None, y_per_group=False):
    b, t, d = x.shape
    kdim = w.shape[0]
    route = router is not None
    ytm = min(tm, OUTPROJ_ROW_GROUP) if y_per_group else tm
    in_specs = [pl.BlockSpec((1, ytm, kdim), lambda bb, i: (bb, i, 0)) for _ in ys]
    in_specs += [pl.BlockSpec((kdim, d), lambda bb, i: (0, 0)),
                 pl.BlockSpec((1, tm, d), lambda bb, i: (bb, i, 0)),
                 pl.BlockSpec((1, d), lambda bb, i: (0, 0)),
                 pl.BlockSpec((1, d), lambda bb, i: (0, 0)),
                 pl.BlockSpec((1, 6, d), lambda bb, i: (mod_row(bb), 0, 0))]
    args = list(ys) + [w, x, gpost.reshape(1, d), gpre.reshape(1, d), mod]
    out_specs = [pl.BlockSpec((1, tm, d), lambda bb, i: (bb, i, 0))]
    out_shape = [jax.ShapeDtypeStruct((b, t, d), F32)]
    if route:
        rw, rb = router
        in_specs += [pl.BlockSpec((d, LANES), lambda bb, i: (0, 0)),
                     pl.BlockSpec((1, LANES), lambda bb, i: (0, 0))]
        args += [rw, rb]
        out_specs += [pl.BlockSpec((1, tm, d // 2 // LANES, 1, LANES), lambda bb, i: (bb, i, 0, 0, 0)),
                      pl.BlockSpec((1, tm, LANES), lambda bb, i: (bb, i, 0))]
        out_shape += [jax.ShapeDtypeStruct((b, t, d // 2 // LANES, 1, LANES), U32),
                      jax.ShapeDtypeStruct((b, t, LANES), F32)]
    else:
        out_specs.append(pl.BlockSpec((1, tm, d), lambda bb, i: (bb, i, 0)))
        out_shape.append(jax.ShapeDtypeStruct((b, t, d), BF16))
    return pl.pallas_call(
        functools.partial(_outproj_kernel, n_y=len(ys), route=route, y_per_group=y_per_group),
        grid=(b, t // tm),
        in_specs=in_specs,
        out_specs=out_specs,
        out_shape=out_shape,
        compiler_params=_cparams(("arbitrary", "arbitrary")),
        name="outproj",
    )(*args)


def _ffn_kernel(h_ref, wgu_ref, wd_ref, x_ref, gpost_ref, mod_ref, o_ref, *, tf):
    h = h_ref[0]
    dff = wd_ref.shape[0]
    acc = None
    for c in range(dff // tf):
        g = jnp.dot(h, wgu_ref[:, c * tf:(c + 1) * tf], preferred_element_type=F32)
        u = jnp.dot(h, wgu_ref[:, dff + c * tf:dff + (c + 1) * tf], preferred_element_type=F32)
        a = (_silu(g) * u).astype(BF16)
        part = jnp.dot(a, wd_ref[c * tf:(c + 1) * tf, :], preferred_element_type=F32)
        acc = part if acc is None else acc + part
    o_ref[0] = x_ref[0] + mod_ref[0, G2:G2 + 1, :] * _rms_rows(acc, gpost_ref[...])


def _ffn(h, w_gu, w_down, x, gpost, mod, mod_row, tm, tf):
    b, t, d = x.shape
    dff = w_down.shape[0]
    return pl.pallas_call(
        functools.partial(_ffn_kernel, tf=tf),
        grid=(b, t // tm),
        in_specs=[pl.BlockSpec((1, tm, d), lambda bb, i: (bb, i, 0)),
                  pl.BlockSpec((d, 2 * dff), lambda bb, i: (0, 0), pipeline_mode=pl.Buffered(1)),
                  pl.BlockSpec((dff, d), lambda bb, i: (0, 0), pipeline_mode=pl.Buffered(1)),
                  pl.BlockSpec((1, tm, d), lambda bb, i: (bb, i, 0)),
                  pl.BlockSpec((1, d), lambda bb, i: (0, 0)),
                  pl.BlockSpec((1, 6, d), lambda bb, i: (mod_row(bb), 0, 0))],
        out_specs=pl.BlockSpec((1, tm, d), lambda bb, i: (bb, i, 0)),
        out_shape=jax.ShapeDtypeStruct((b, t, d), F32),
        compiler_params=_cparams(("arbitrary", "arbitrary")),
        name="ffn",
    )(h, w_gu, w_down, x, gpost.reshape(1, d), mod)


def _attn_kernel(q_ref, kl_ref, vl_ref, kc_ref, vc_ref, lam_ref, subln_ref, oe_ref, oo_ref, ks, vs, sa, sb, ma, mb,
                 *, lambda_init, nstep, tq):
    i = pl.program_id(2)
    t = kl_ref.shape[1]

    @pl.when(i == 0)
    def _():
        ks[0:t, :] = kl_ref[0]
        ks[t:, :] = kc_ref[0]
        vs[0:t, 0:LANES] = vl_ref[0]
        vs[t:, 0:LANES] = vc_ref[0]
        vs[:, LANES:] = jnp.ones((vs.shape[0], LANES), BF16)

    def scores(half, s_scr, m_scr):
        q = q_ref[0, half * tq:(half + 1) * tq, :]
        k = ks[...]
        lane = lax.broadcasted_iota(I32, q.shape, 1)
        zero = jnp.zeros_like(q)
        for h, qh in enumerate((jnp.where(lane < DIFF_HEAD_DIM, q, zero),
                                jnp.where(lane >= DIFF_HEAD_DIM, q, zero))):
            s = lax.dot_general(qh, k, (((1,), (1,)), ((), ())), preferred_element_type=F32)
            s_scr[h] = s
            m_scr[h] = jnp.max(s, axis=-1, keepdims=True)

    def finish(s_scr, m_scr, o_ref):
        v = vs[...]
        lp = lam_ref[...]
        lam = (jnp.exp(jnp.sum(lp[0:1] * lp[1:2], axis=-1, keepdims=True))
               - jnp.exp(jnp.sum(lp[2:3] * lp[3:4], axis=-1, keepdims=True)) + lambda_init)
        outs = []
        for h in range(2):
            p = jnp.exp2(s_scr[h] - m_scr[h]).astype(BF16)
            r = jnp.dot(p, v, preferred_element_type=F32)
            outs.append(r[:, 0:LANES] / r[:, LANES:LANES + 1])
        o = outs[0] - lam * outs[1]
        o_ref[0] = (_rms_rows(o, subln_ref[...]) * (1.0 - lambda_init)).astype(o_ref.dtype)

    @pl.when(i == 0)
    def _():
        scores(0, sa, ma)

    @pl.when(jnp.logical_and(i > 0, i < nstep))
    def _():
        scores(0, sa, ma)
        finish(sb, mb, oo_ref)

    @pl.when(i < nstep)
    def _():
        scores(1, sb, mb)
        finish(sa, ma, oe_ref)

    @pl.when(i == nstep)
    def _():
        finish(sb, mb, oo_ref)


def _attention(z, zc, lam_params, subln, tq, lambda_init):
    b, t, n3 = z.shape
    tc = zc.shape[1]
    d = n3 // 3
    npair = d // LANES
    nstep = t // (2 * tq)
    s_scr = pltpu.VMEM((2, tq, t + tc), F32)
    m_scr = pltpu.VMEM((2, tq, 1), F32)
    half = jax.ShapeDtypeStruct((b, t // 2, d), BF16)
    return pl.pallas_call(
        functools.partial(_attn_kernel, lambda_init=lambda_init, nstep=nstep, tq=tq),
        grid=(b, npair, nstep + 1),
        in_specs=[pl.BlockSpec((1, 2 * tq, LANES), lambda bb, p, i: (bb, jnp.minimum(i, nstep - 1), p)),
                  pl.BlockSpec((1, t, LANES), lambda bb, p, i: (bb, 0, npair + p)),
                  pl.BlockSpec((1, t, LANES), lambda bb, p, i: (bb, 0, 2 * npair + p)),
                  pl.BlockSpec((1, tc, LANES), lambda bb, p, i: (bb, 0, npair + p)),
                  pl.BlockSpec((1, tc, LANES), lambda bb, p, i: (bb, 0, 2 * npair + p)),
                  pl.BlockSpec(lam_params.shape, lambda bb, p, i: (0, 0)),
                  pl.BlockSpec((1, LANES), lambda bb, p, i: (0, 0))],
        out_specs=[pl.BlockSpec((1, tq, LANES), lambda bb, p, i: (bb, jnp.minimum(i, nstep - 1), p)),
                   pl.BlockSpec((1, tq, LANES), lambda bb, p, i: (bb, jnp.maximum(i - 1, 0), p))],
        out_shape=[half, half],
        scratch_shapes=[pltpu.VMEM((t + tc, LANES), BF16), pltpu.VMEM((t + tc, 2 * LANES), BF16),
                        s_scr, s_scr, m_scr, m_scr],
        compiler_params=_cparams(("arbitrary", "arbitrary", "arbitrary")),
        name="attention",
    )(z, z, z, zc, zc, lam_params.astype(F32), subln.reshape(1, LANES).astype(F32))


def _moe_kernel(te_ref, tv_ref, pinv_ref, inv_ref, ninv_ref, wg_ref, wu_ref, wd_ref, h_hbm,
                y_hbm, gbuf, hbuf, acc, obuf, gsem, ssem, *, tm, n_tok):
    del te_ref
    j = pl.program_id(0)
    f = pl.program_id(1)
    nf = pl.num_programs(1)
    slot = j % 2
    other = 1 - slot
    rows_per_step = tm // MOE_HIDDEN_STEPS
    valid = tv_ref[j] > 0
    drain = jnp.logical_and(jnp.logical_not(valid), tv_ref[jnp.maximum(j - 1, 0)] * (j > 0).astype(I32) > 0)

    def gather_row(idx_ref, to_slot, g, k):
        tok = jnp.minimum(idx_ref[0, 0, g * SUBLANES + k] >> 1, n_tok - 1)
        pltpu.make_async_copy(h_hbm.at[tok], gbuf.at[to_slot, :, g, pl.ds(k, 1), :], gsem.at[to_slot]).start()

    def scatter_row(idx_ref, from_slot, g, k):
        r = g * SUBLANES + k
        a = idx_ref[0, 0, r]
        row = jnp.where(a < 2 * n_tok, (a & 1) * n_tok + (a >> 1), 2 * n_tok + r)
        pltpu.make_async_copy(obuf.at[from_slot, :, g, pl.ds(k, 1), :], y_hbm.at[:, pl.ds(row, 1), :],
                              ssem.at[from_slot]).start()

    def loop_rows(fn):
        def body(g, c):
            for k in range(SUBLANES):
                fn(g, k)
            return c
        lax.fori_loop(0, tm // SUBLANES, body, 0)

    def wait_gather(s):
        pltpu.make_async_copy(gbuf.at[1 - s], gbuf.at[s], gsem.at[s]).wait()

    def wait_scatter(s):
        pltpu.make_async_copy(obuf.at[1 - s], obuf.at[s], ssem.at[s]).wait()

    @pl.when(jnp.logical_and(j == 0, f == 0))
    def _():
        obuf[1] = jnp.zeros(obuf.shape[1:], U32)
        loop_rows(lambda g, k: gather_row(inv_ref, 0, g, k))

    @pl.when(jnp.logical_and(jnp.logical_or(valid, drain), f == 0))
    def _():
        wait_gather(slot)

    @pl.when(jnp.logical_and(valid, f == 0))
    def _():
        n_slab = gbuf.shape[1]
        for sl in range(n_slab):
            lo, hi = _unpack_rows(gbuf[slot, sl].reshape(tm, LANES))
            hbuf[:, sl * LANES:(sl + 1) * LANES] = lo.astype(BF16)
            hbuf[:, (n_slab + sl) * LANES:(n_slab + sl + 1) * LANES] = hi.astype(BF16)

    @pl.when(valid)
    def _():
        base = f * (rows_per_step // SUBLANES)
        for k in range(rows_per_step):
            gather_row(ninv_ref, other, base + k // SUBLANES, k % SUBLANES)
            scatter_row(pinv_ref, other, base + k // SUBLANES, k % SUBLANES)
        h = hbuf[...]
        g = jnp.dot(h, wg_ref[0], preferred_element_type=F32)
        u = jnp.dot(h, wu_ref[0], preferred_element_type=F32)
        a = (_silu(g) * u).astype(BF16)
        part = jnp.dot(a, wd_ref[0], preferred_element_type=F32)

        @pl.when(f == 0)
        def _():
            acc[...] = part

        @pl.when(f > 0)
        def _():
            acc[...] += part

    @pl.when(jnp.logical_and(valid, f == nf - 1))
    def _():
        @pl.when(j > 0)
        def _():
            wait_scatter(slot)

        packed = _pack_rows(acc[...])
        for sl in range(obuf.shape[1]):
            obuf[slot, sl] = packed[:, sl * LANES:(sl + 1) * LANES].reshape(tm // SUBLANES, SUBLANES, LANES)

    @pl.when(jnp.logical_and(drain, f == 0))
    def _():
        loop_rows(lambda g, k: scatter_row(pinv_ref, other, g, k))
        wait_scatter(other)
        wait_scatter(slot)


def _moe(hp, te, tv, inv, w_gu, w_down, tm):
    nt = te.shape[0]
    n_tok, n_slab = hp.shape[:2]
    e, d, two_ff = w_gu.shape
    dff = two_ff // 2
    nf = MOE_HIDDEN_STEPS
    tf = dff // nf

    def wf(j, f, te_r, tv_r):
        return jnp.where(tv_r[j] > 0, f, nf - 1)

    def inv_spec(off):
        return pl.BlockSpec((1, 1, tm), lambda j, f, a, c: (jnp.minimum(j + off, nt), 0, 0),
                            memory_space=pltpu.SMEM)

    grid_spec = pltpu.PrefetchScalarGridSpec(
        num_scalar_prefetch=2,
        grid=(nt, nf),
        in_specs=[inv_spec(0), inv_spec(1), inv_spec(2),
                  pl.BlockSpec((1, d, tf), lambda j, f, a, c: (a[j], 0, wf(j, f, a, c))),
                  pl.BlockSpec((1, d, tf), lambda j, f, a, c: (a[j], 0, wf(j, f, a, c) + nf)),
                  pl.BlockSpec((1, tf, d), lambda j, f, a, c: (a[j], wf(j, f, a, c), 0)),
                  pl.BlockSpec(memory_space=pl.ANY)],
        out_specs=pl.BlockSpec(memory_space=pl.ANY),
        scratch_shapes=[pltpu.VMEM((2, n_slab, tm // SUBLANES, SUBLANES, LANES), U32),
                        pltpu.VMEM((tm, d), BF16),
                        pltpu.VMEM((tm, d), F32),
                        pltpu.VMEM((2, n_slab, tm // SUBLANES, SUBLANES, LANES), U32),
                        pltpu.SemaphoreType.DMA((2,)),
                        pltpu.SemaphoreType.DMA((2,))],
    )
    return pl.pallas_call(
        functools.partial(_moe_kernel, tm=tm, n_tok=n_tok),
        grid_spec=grid_spec,
        out_shape=jax.ShapeDtypeStruct((n_slab, 2 * n_tok + tm, LANES), U32),
        compiler_params=_cparams(("arbitrary", "arbitrary")),
        name="moe",
    )(te, tv, inv, inv, inv, w_gu, w_gu, w_down, hp)


def _route_plan(rt, tm, nt):
    t = rt.shape[0]
    e_flat = rt[:, 0:2].astype(I32).reshape(-1)
    onehot = (e_flat[:, None] == jnp.arange(N_EXPERTS, dtype=I32)[None, :]).astype(I32)
    csum = jnp.cumsum(onehot, axis=0)
    rank = jnp.sum(csum * onehot, axis=1) - 1
    counts = csum[-1]
    tiles_e = (counts + tm - 1) // tm
    tile_end = jnp.cumsum(tiles_e)
    row_off = (tile_end - tiles_e) * tm
    dest = jnp.sum(row_off[None, :] * onehot, axis=1) + rank
    inv = jnp.full((nt * tm,), 2 * t, I32).at[dest].set(jnp.arange(2 * t, dtype=I32))
    n_used = tile_end[-1]
    tid = jnp.arange(nt, dtype=I32)
    tv = (tid < n_used).astype(I32)
    te = jnp.sum((jnp.minimum(tid, n_used - 1)[:, None] >= tile_end[None, :]).astype(I32), axis=1)
    inv = jnp.concatenate([jnp.full((tm,), 2 * t, I32), inv])
    return jnp.minimum(te, N_EXPERTS - 1), tv, inv.reshape(nt + 1, 1, tm)


def _combine_kernel(y0_ref, y1_ref, rt_ref, x_ref, gpost_ref, mod_ref, o_ref):
    w0 = rt_ref[0, :, 2:3]
    w1 = rt_ref[0, :, 3:4]
    los, his = [], []
    for sl in range(y0_ref.shape[0]):
        lo0, hi0 = _unpack_rows(y0_ref[sl])
        lo1, hi1 = _unpack_rows(y1_ref[sl])
        los.append(w0 * lo0 + w1 * lo1)
        his.append(w0 * hi0 + w1 * hi1)
    y = jnp.concatenate(los + his, axis=1)
    o_ref[0] = x_ref[0] + mod_ref[0, G2:G2 + 1, :] * _rms_rows(y, gpost_ref[...])


def _combine(y2, rt, x, gpost, mod, tm):
    b, t, d = x.shape
    per_b = t // tm
    nblk = b * per_b
    return pl.pallas_call(
        _combine_kernel,
        grid=(b, per_b),
        in_specs=[pl.BlockSpec((d // 2 // LANES, tm, LANES), lambda bb, i: (0, bb * per_b + i, 0)),
                  pl.BlockSpec((d // 2 // LANES, tm, LANES), lambda bb, i: (0, nblk + bb * per_b + i, 0)),
                  pl.BlockSpec((1, tm, LANES), lambda bb, i: (bb, i, 0)),
                  pl.BlockSpec((1, tm, d), lambda bb, i: (bb, i, 0)),
                  pl.BlockSpec((1, d), lambda bb, i: (0, 0)),
                  pl.BlockSpec((1, 6, d), lambda bb, i: (bb, 0, 0))],
        out_specs=pl.BlockSpec((1, tm, d), lambda bb, i: (bb, i, 0)),
        out_shape=jax.ShapeDtypeStruct((b, t, d), F32),
        compiler_params=_cparams(("arbitrary", "arbitrary")),
        name="combine",
    )(y2, y2, rt, x, gpost.reshape(1, d), mod)


def _rope_angles(rows, head_dim):
    pos_r = jnp.repeat(jnp.arange(rows, dtype=F32), GRID_W)
    pos_c = jnp.tile(jnp.arange(GRID_W, dtype=F32), rows)
    axis_dim = head_dim // 2
    freqs = ROPE_BASE ** (-jnp.arange(0, axis_dim, 2, dtype=F32) / axis_dim)
    return pos_r[:, None] * freqs[None, :], pos_c[:, None] * freqs[None, :]


def _rope_tables(rows, head_dim, reps):
    ar, ac = _rope_angles(rows, head_dim)
    cos = jnp.concatenate([jnp.cos(ar), jnp.cos(ar), jnp.cos(ac), jnp.cos(ac)], axis=1)
    zr = jnp.zeros_like(ar)
    s_up = jnp.concatenate([-jnp.sin(ar), zr, -jnp.sin(ac), zr], axis=1)
    s_dn = jnp.concatenate([zr, jnp.sin(ar), zr, jnp.sin(ac)], axis=1)
    return jnp.stack([jnp.tile(cos, (1, reps)), jnp.tile(s_up, (1, reps)), jnp.tile(s_dn, (1, reps))])


def _identity_tables(n_tab, tm, period):
    return jnp.concatenate([jnp.ones((1, tm, period), F32), jnp.zeros((n_tab - 1, tm, period), F32)])


def _tile(t, want):
    return min(t, want)


def kernel(x, c, ctx, c_ctx, l0_w_mod, l0_b_mod, l0_norm_pre_mix, l0_norm_post_mix, l0_norm_pre_ffn, l0_norm_post_ffn, l0_ret_w_in, l0_ret_decay_logit, l0_ret_w_out, l0_ffn_w_gate_up, l0_ffn_w_down, l1_w_mod, l1_b_mod, l1_norm_pre_mix, l1_norm_post_mix, l1_norm_pre_ffn, l1_norm_post_ffn, l1_attn_w_in, l1_attn_lambda, l1_attn_subln, l1_attn_w_out, l1_router_w, l1_router_b, l1_moe_w_gate_up, l1_moe_w_down):
    b, t, d = x.shape
    tc = ctx.shape[1]
    rows = t // GRID_W
    assert t % tc == 0

    n_rows = -(-(b + 1) // 8) * 8
    cc = jnp.concatenate([c, c_ctx[None, :], jnp.zeros((n_rows - b - 1, d), F32)], axis=0)
    lat_row = lambda bb: bb
    ctx_row = lambda bb: b

    tm = 2 * OUTPROJ_ROW_GROUP
    tn = 1024
    assert t % tm == 0

    mod0 = _modulation(cc, l0_w_mod, l0_b_mod)
    w_in0 = l0_ret_w_in.astype(BF16)
    dk = w_in0.shape[1] // (8 * RET_HEADS)
    tab_ret = _rope_tables(rows, dk, 1)
    tab_ret = jnp.stack([tab_ret[0], tab_ret[1] + tab_ret[2]])
    ret_scales = (1.0,) * (RET_HEADS * dk // tn) + (dk ** -0.5,) * (RET_HEADS * dk // tn)
    z0 = _inproj(x, l0_norm_pre_mix, mod0, lat_row, w_in0, tab_ret, True, tm, tn, ret_scales, (dk // 4,))
    zc0 = _inproj(ctx, l0_norm_pre_mix, mod0, ctx_row, w_in0, _identity_tables(2, tc, dk), False, tc, tn,
                  ret_scales, (dk // 4,))
    yf, yb, yfc, ybc = _retention(z0, zc0, l0_ret_decay_logit)

    w_out0 = l0_ret_w_out.astype(BF16)
    x1, h1 = _outproj([yf, yb], w_out0, x, l0_norm_post_mix, l0_norm_pre_ffn, mod0, lat_row, tm)
    c1, hc1 = _outproj([yfc, ybc], w_out0, ctx, l0_norm_post_mix, l0_norm_pre_ffn, mod0, ctx_row, tc)

    w_gu0 = l0_ffn_w_gate_up.astype(BF16)
    w_dn0 = l0_ffn_w_down.astype(BF16)
    x2 = _ffn(h1, w_gu0, w_dn0, x1, l0_norm_post_ffn, mod0, lat_row, tm, FFN_HIDDEN_TILE)
    c2 = _ffn(hc1, w_gu0, w_dn0, c1, l0_norm_post_ffn, mod0, ctx_row, tc, FFN_HIDDEN_TILE)

    lambda_init = 0.8 - 0.6 * math.exp(-0.3 * 1)
    mod1 = _modulation(cc, l1_w_mod, l1_b_mod)
    w_in1 = l1_attn_w_in.astype(BF16)
    tab_diff = _rope_tables(rows, DIFF_HEAD_DIM, LANES // DIFF_HEAD_DIM)
    q4 = DIFF_HEAD_DIM // 4
    diff_scales = (DIFF_HEAD_DIM ** -0.5 * LOG2_E,) * (d // tn) + (1.0,) * (d // tn)
    z1 = _inproj(x2, l1_norm_pre_mix, mod1, lat_row, w_in1, tab_diff, True, tm, tn, diff_scales,
                 (LANES - q4, q4))
    zc1 = _inproj(c2, l1_norm_pre_mix, mod1, ctx_row, w_in1, _identity_tables(3, tc, LANES), False, tc, tn,
                  (1.0,) * (2 * d // tn), (LANES - q4, q4))
    o_even, o_odd = _attention(z1, zc1, l1_attn_lambda, l1_attn_subln, OUTPROJ_ROW_GROUP, lambda_init)

    rw = jnp.zeros((d, LANES), F32).at[:, :N_EXPERTS].set(l1_router_w)
    rb = jnp.full((1, LANES), NEG_BIG, F32).at[0, :N_EXPERTS].set(l1_router_b)
    x3, hp, rt = _outproj([o_even, o_odd], l1_attn_w_out.astype(BF16), x2, l1_norm_post_mix, l1_norm_pre_ffn,
                          mod1, lat_row, tm, router=(rw, rb), y_per_group=True)

    n_tok = b * t
    tm_e = _tile(n_tok, 1024)
    nt = 2 * n_tok // tm_e + N_EXPERTS
    te, tv, inv = _route_plan(rt.reshape(n_tok, LANES), tm_e, nt)
    y2 = _moe(hp.reshape(n_tok, d // 2 // LANES, 1, LANES), te, tv, inv, l1_moe_w_gate_up.astype(BF16),
              l1_moe_w_down.astype(BF16), tm_e)
    return _combine(y2, rt, x3, l1_norm_post_ffn, mod1, tm)
```

```python
import functools
import math

import jax
import jax.numpy as jnp
from jax import lax
from jax.experimental import pallas as pl
from jax.experimental.pallas import tpu as pltpu

F32 = jnp.float32
BF16 = jnp.bfloat16
U32 = jnp.uint32
I32 = jnp.int32

EPS = 1e-6
GRID_W = 64
ROPE_BASE = 10000.0
RET_HEADS = 4
DIFF_HEADS = 8
DIFF_HEAD_DIM = 64
N_EXPERTS = 8
LANES = 128
SUBLANES = 8
NEG_BIG = -1e30
VMEM_LIMIT = 56 * 1024 * 1024
DMA_UNROLL = 8
MOE_HIDDEN_STEPS = 4
FFN_HIDDEN_TILE = 256
RET_CHUNKS_PER_STEP = 4
OUTPROJ_ROW_GROUP = 256
LOG2_E = 1.4426950408889634

SH1, SC1, G1, SH2, SC2, G2 = range(6)


def _cparams(sem):
    return pltpu.CompilerParams(dimension_semantics=sem, vmem_limit_bytes=VMEM_LIMIT)


def _rms_rows(x, gain):
    return x * lax.rsqrt(jnp.mean(x * x, axis=-1, keepdims=True) + EPS) * gain


def _silu(x):
    return x * jax.nn.sigmoid(x)


def _mod_kernel(c_ref, w_ref, b_ref, o_ref):
    a = _silu(c_ref[...])
    o_ref[...] = jnp.dot(a, w_ref[...], preferred_element_type=F32,
                         precision=lax.Precision.HIGHEST) + b_ref[...]


def _modulation(cc, w_mod, b_mod):
    r, d = cc.shape
    n = w_mod.shape[1]
    tn = n // 6
    out = pl.pallas_call(
        _mod_kernel,
        grid=(n // tn,),
        in_specs=[pl.BlockSpec((r, d), lambda j: (0, 0)),
                  pl.BlockSpec((d, tn), lambda j: (0, j)),
                  pl.BlockSpec((1, tn), lambda j: (0, j))],
        out_specs=pl.BlockSpec((r, tn), lambda j: (0, j)),
        out_shape=jax.ShapeDtypeStruct((r, n), F32),
        compiler_params=_cparams(("arbitrary",)),
        name="mod",
    )(cc, w_mod, b_mod.reshape(1, n))
    return out.reshape(r, 6, d)


def _inproj_kernel(x_ref, g_ref, mod_ref, w_ref, tab_ref, o_ref, *, tn, rope_scales, shifts, period):
    y = _rms_rows(x_ref[0], g_ref[...])
    h = (y * (1.0 + mod_ref[0, SC1:SC1 + 1, :]) + mod_ref[0, SH1:SH1 + 1, :]).astype(BF16)
    for j in range(w_ref.shape[1] // tn):
        acc = jnp.dot(h, w_ref[:, j * tn:(j + 1) * tn], preferred_element_type=F32)
        if j >= len(rope_scales):
            o_ref[0, :, j * tn:(j + 1) * tn] = acc.astype(o_ref.dtype)
            continue
        for k in range(tn // LANES):
            xs = acc[:, k * LANES:(k + 1) * LANES]
            p = (k * LANES) % period
            r = xs * tab_ref[0, :, p:p + LANES]
            for si, sft in enumerate(shifts):
                r = r + pltpu.roll(xs, sft, 1) * tab_ref[1 + si, :, p:p + LANES]
            if rope_scales[j] != 1.0:
                r = r * rope_scales[j]
            c0 = j * tn + k * LANES
            o_ref[0, :, c0:c0 + LANES] = r.astype(o_ref.dtype)


def _inproj(x, gain, mod, mod_row, w, tab, tab_per_tile, tm, tn, rope_scales, shifts):
    b, t, d = x.shape
    n = w.shape[1]
    period = tab.shape[2]
    kern = functools.partial(_inproj_kernel, tn=tn, rope_scales=rope_scales, shifts=shifts, period=period)
    tab_map = (lambda bb, i: (0, i, 0)) if tab_per_tile else (lambda bb, i: (0, 0, 0))
    return pl.pallas_call(
        kern,
        grid=(b, t // tm),
        in_specs=[pl.BlockSpec((1, tm, d), lambda bb, i: (bb, i, 0)),
                  pl.BlockSpec((1, d), lambda bb, i: (0, 0)),
                  pl.BlockSpec((1, 6, d), lambda bb, i: (mod_row(bb), 0, 0)),
                  pl.BlockSpec((d, n), lambda bb, i: (0, 0), pipeline_mode=pl.Buffered(1)),
                  pl.BlockSpec((tab.shape[0], tm, period), tab_map)],
        out_specs=pl.BlockSpec((1, tm, n), lambda bb, i: (bb, i, 0)),
        out_shape=jax.ShapeDtypeStruct((b, t, n), BF16),
        compiler_params=_cparams(("arbitrary", "arbitrary")),
        name="inproj",
    )(x, gain.reshape(1, d), mod, w, tab)


def _ret_kernel(qf, kf, vf, gf, qb, kb, vb, gb, qc, kc, vc, gfc, gbc, intra, qdec, kdec, cdec,
                yf, yb, yfc, ybc, st):
    s = pl.program_id(2)
    chunk = qc.shape[1]

    def direction(d, q_ref, k_ref, v_ref, g_ref, y_ref, sub):
        rs = slice(sub * chunk, (sub + 1) * chunk)
        q = q_ref[0, rs, :]
        k = k_ref[0, rs, :]
        v = v_ref[0, rs, :]
        scores = lax.dot_general(q, k, (((1,), (1,)), ((), ())), preferred_element_type=F32) * intra[d, 0]
        state = st[d]
        o = (jnp.dot(scores.astype(BF16), v, preferred_element_type=F32)
             + qdec[d, 0] * jnp.dot(q, state.astype(BF16), preferred_element_type=F32))
        ks = (k.astype(F32) * kdec[d, 0]).astype(BF16)
        st[d] = state * cdec[d, 0] + lax.dot_general(ks, v, (((0,), (0,)), ((), ())),
                                                     preferred_element_type=F32)
        hn = o * lax.rsqrt(jnp.mean(o * o, axis=-1, keepdims=True) + EPS)
        y_ref[0, rs, :] = (_silu(g_ref[0, rs, :].astype(F32)) * hn).astype(y_ref.dtype)

    @pl.when(s == 0)
    def _():
        st[...] = jnp.zeros(st.shape, F32)
        direction(0, qc, kc, vc, gfc, yfc, 0)
        direction(1, qc, kc, vc, gbc, ybc, 0)

    @pl.when(s > 0)
    def _():
        for c in range(RET_CHUNKS_PER_STEP):
            direction(0, qf, kf, vf, gf, yf, c)
            direction(1, qb, kb, vb, gb, yb, RET_CHUNKS_PER_STEP - 1 - c)


def _retention(z, zc, decay_logit):
    b, t, _ = z.shape
    chunk = zc.shape[1]
    n_lat = t // chunk
    h = RET_HEADS
    dk = z.shape[2] // (8 * h)
    dv = 2 * dk

    log_g = jax.nn.log_sigmoid(decay_logit.astype(F32))
    idx = jnp.arange(chunk, dtype=F32)
    diff = idx[:, None] - idx[None, :]
    lg = log_g[:, :, None, None]
    intra_f = jnp.where(diff >= 0, jnp.exp(lg[0] * jnp.maximum(diff, 0.0)), 0.0)
    intra_b = jnp.where(diff <= 0, jnp.exp(lg[1] * jnp.maximum(-diff, 0.0)), 0.0)
    intra = jnp.stack([intra_f, intra_b])
    qdec = jnp.stack([jnp.exp(log_g[0][:, None] * (idx + 1.0)),
                      jnp.exp(log_g[1][:, None] * (chunk - idx))])[..., None]
    kdec = jnp.stack([jnp.exp(log_g[0][:, None] * (chunk - 1.0 - idx)),
                      jnp.exp(log_g[1][:, None] * idx)])[..., None]
    cdec = jnp.exp(log_g * chunk)[..., None, None]

    cps = RET_CHUNKS_PER_STEP
    assert n_lat % cps == 0
    n_blk = n_lat // cps

    def fwd(s):
        return jnp.maximum(s - 1, 0)

    def bwd(s):
        return n_blk - jnp.maximum(s, 1)

    def spec(width, col0, chunk_of):
        return pl.BlockSpec((1, cps * chunk, width), lambda bb, hh, s: (bb, chunk_of(s), col0 + hh))

    def cspec(width, col0):
        return pl.BlockSpec((1, chunk, width), lambda bb, hh, s: (bb, 0, col0 + hh))

    in_specs = [spec(dk, 0, fwd), spec(dk, h, fwd), spec(dv, h, fwd), spec(dv, 2 * h, fwd),
                spec(dk, 0, bwd), spec(dk, h, bwd), spec(dv, h, bwd), spec(dv, 3 * h, bwd),
                cspec(dk, 0), cspec(dk, h), cspec(dv, h), cspec(dv, 2 * h), cspec(dv, 3 * h),
                pl.BlockSpec((2, 1, chunk, chunk), lambda bb, hh, s: (0, hh, 0, 0)),
                pl.BlockSpec((2, 1, chunk, 1), lambda bb, hh, s: (0, hh, 0, 0)),
                pl.BlockSpec((2, 1, chunk, 1), lambda bb, hh, s: (0, hh, 0, 0)),
                pl.BlockSpec((2, 1, 1, 1), lambda bb, hh, s: (0, hh, 0, 0))]
    out_specs = [pl.BlockSpec((1, cps * chunk, dv), lambda bb, hh, s: (bb, fwd(s), hh)),
                 pl.BlockSpec((1, cps * chunk, dv), lambda bb, hh, s: (bb, bwd(s), hh)),
                 pl.BlockSpec((1, chunk, dv), lambda bb, hh, s: (bb, 0, hh)),
                 pl.BlockSpec((1, chunk, dv), lambda bb, hh, s: (bb, 0, hh))]
    yl = jax.ShapeDtypeStruct((b, t, h * dv), BF16)
    yc = jax.ShapeDtypeStruct((b, chunk, h * dv), BF16)
    return pl.pallas_call(
        _ret_kernel,
        grid=(b, h, n_blk + 1),
        in_specs=in_specs,
        out_specs=out_specs,
        out_shape=[yl, yl, yc, yc],
        scratch_shapes=[pltpu.VMEM((2, dk, dv), F32)],
        compiler_params=_cparams(("arbitrary", "arbitrary", "arbitrary")),
        name="retention",
    )(z, z, z, z, z, z, z, z, zc, zc, zc, zc, zc, intra, qdec, kdec, cdec)


def _pack_rows(h):
    n = h.shape[1] // 2
    bits = lax.bitcast_convert_type(h.astype(BF16).astype(F32), U32)
    return (bits[:, :n] >> 16) | (bits[:, n:] & jnp.uint32(0xFFFF0000))


def _unpack_rows(w):
    lo = lax.bitcast_convert_type(w << 16, F32)
    hi = lax.bitcast_convert_type(w & jnp.uint32(0xFFFF0000), F32)
    return lo, hi


def _outproj_kernel(*refs, n_y, route, y_per_group):
    y_refs = refs[:n_y]
    w_ref, x_ref, gpost_ref, gpre_ref, mod_ref = refs[n_y:n_y + 5]
    rest = refs[n_y + 5:]
    if route:
        rw_ref, rb_ref, xo_ref, hp_ref, rt_ref = rest
    else:
        xo_ref, h_ref = rest
    tm = x_ref.shape[1]
    rows = min(tm, OUTPROJ_ROW_GROUP)
    for c in range(tm // rows):
        rs = slice(c * rows, (c + 1) * rows)
        if y_per_group:
            y = y_refs[c][0]
        else:
            y = y_refs[0][0, rs, :]
            if n_y == 2:
                y = (y.astype(F32) + y_refs[1][0, rs, :].astype(F32)).astype(BF16)
        o = jnp.dot(y, w_ref[...], preferred_element_type=F32)
        xn = x_ref[0, rs, :] + mod_ref[0, G1:G1 + 1, :] * _rms_rows(o, gpost_ref[...])
        xo_ref[0, rs, :] = xn
        h = _rms_rows(xn, gpre_ref[...]) * (1.0 + mod_ref[0, SC2:SC2 + 1, :]) + mod_ref[0, SH2:SH2 + 1, :]
        if not route:
            h_ref[0, rs, :] = h.astype(h_ref.dtype)
            continue
        packed = _pack_rows(h)
        for sl in range(packed.shape[1] // LANES):
            hp_ref[0, rs, sl, 0, :] = packed[:, sl * LANES:(sl + 1) * LANES]
        logits = jnp.dot(h, rw_ref[...], preferred_element_type=F32,
                         precision=lax.Precision.HIGHEST) + rb_ref[...]
        lane = lax.broadcasted_iota(I32, logits.shape, 1).astype(F32)
        m1 = jnp.max(logits, axis=-1, keepdims=True)
        i1 = jnp.min(jnp.where(logits == m1, lane, float(LANES)), axis=-1, keepdims=True)
        masked = jnp.where(lane == i1, NEG_BIG, logits)
        m2 = jnp.max(masked, axis=-1, keepdims=True)
        i2 = jnp.min(jnp.where(masked == m2, lane, float(LANES)), axis=-1, keepdims=True)
        e2 = jnp.exp(m2 - m1)
        den = 1.0 + e2
        w1 = 1.0 / den
        w2 = e2 / den
        rt_ref[0, rs, :] = jnp.where(lane == 0.0, i1,
                                     jnp.where(lane == 1.0, i2,
                                               jnp.where(lane == 2.0, w1, jnp.where(lane == 3.0, w2, 0.0))))


def _outproj(ys, w, x, gpost, gpre, mod, mod_row, tm, router=None, y_per_group=False):
    b, t, d = x.shape
    kdim = w.shape[0]
    route = router is not None
    ytm = min(tm, OUTPROJ_ROW_GROUP) if y_per_group else tm
    in_specs = [pl.BlockSpec((1, ytm, kdim), lambda bb, i: (bb, i, 0)) for _ in ys]
    in_specs += [pl.BlockSpec((kdim, d), lambda bb, i: (0, 0)),
                 pl.BlockSpec((1, tm, d), lambda bb, i: (bb, i, 0)),
                 pl.BlockSpec((1, d), lambda bb, i: (0, 0)),
                 pl.BlockSpec((1, d), lambda bb, i: (0, 0)),
                 pl.BlockSpec((1, 6, d), lambda bb, i: (mod_row(bb), 0, 0))]
    args = list(ys) + [w, x, gpost.reshape(1, d), gpre.reshape(1, d), mod]
    out_specs = [pl.BlockSpec((1, tm, d), lambda bb, i: (bb, i, 0))]
    out_shape = [jax.ShapeDtypeStruct((b, t, d), F32)]
    if route:
        rw, rb = router
        in_specs += [pl.BlockSpec((d, LANES), lambda bb, i: (0, 0)),
                     pl.BlockSpec((1, LANES), lambda bb, i: (0, 0))]
        args += [rw, rb]
        out_specs += [pl.BlockSpec((1, tm, d // 2 // LANES, 1, LANES), lambda bb, i: (bb, i, 0, 0, 0)),
                      pl.BlockSpec((1, tm, LANES), lambda bb, i: (bb, i, 0))]
        out_shape += [jax.ShapeDtypeStruct((b, t, d // 2 // LANES, 1, LANES), U32),
                      jax.ShapeDtypeStruct((b, t, LANES), F32)]
    else:
        out_specs.append(pl.BlockSpec((1, tm, d), lambda bb, i: (bb, i, 0)))
        out_shape.append(jax.ShapeDtypeStruct((b, t, d), BF16))
    return pl.pallas_call(
        functools.partial(_outproj_kernel, n_y=len(ys), route=route, y_per_group=y_per_group),
        grid=(b, t // tm),
        in_specs=in_specs,
        out_specs=out_specs,
        out_shape=out_shape,
        compiler_params=_cparams(("arbitrary", "arbitrary")),
        name="outproj",
    )(*args)


def _ffn_kernel(h_ref, wgu_ref, wd_ref, x_ref, gpost_ref, mod_ref, o_ref, *, tf):
    h = h_ref[0]
    dff = wd_ref.shape[0]
    acc = None
    for c in range(dff // tf):
        g = jnp.dot(h, wgu_ref[:, c * tf:(c + 1) * tf], preferred_element_type=F32)
        u = jnp.dot(h, wgu_ref[:, dff + c * tf:dff + (c + 1) * tf], preferred_element_type=F32)
        a = (_silu(g) * u).astype(BF16)
        part = jnp.dot(a, wd_ref[c * tf:(c + 1) * tf, :], preferred_element_type=F32)
        acc = part if acc is None else acc + part
    o_ref[0] = x_ref[0] + mod_ref[0, G2:G2 + 1, :] * _rms_rows(acc, gpost_ref[...])


def _ffn(h, w_gu, w_down, x, gpost, mod, mod_row, tm, tf):
    b, t, d = x.shape
    dff = w_down.shape[0]
    return pl.pallas_call(
        functools.partial(_ffn_kernel, tf=tf),
        grid=(b, t // tm),
        in_specs=[pl.BlockSpec((1, tm, d), lambda bb, i: (bb, i, 0)),
                  pl.BlockSpec((d, 2 * dff), lambda bb, i: (0, 0), pipeline_mode=pl.Buffered(1)),
                  pl.BlockSpec((dff, d), lambda bb, i: (0, 0), pipeline_mode=pl.Buffered(1)),
                  pl.BlockSpec((1, tm, d), lambda bb, i: (bb, i, 0)),
                  pl.BlockSpec((1, d), lambda bb, i: (0, 0)),
                  pl.BlockSpec((1, 6, d), lambda bb, i: (mod_row(bb), 0, 0))],
        out_specs=pl.BlockSpec((1, tm, d), lambda bb, i: (bb, i, 0)),
        out_shape=jax.ShapeDtypeStruct((b, t, d), F32),
        compiler_params=_cparams(("arbitrary", "arbitrary")),
        name="ffn",
    )(h, w_gu, w_down, x, gpost.reshape(1, d), mod)


def _attn_kernel(q_ref, kl_ref, vl_ref, kc_ref, vc_ref, lam_ref, subln_ref, oe_ref, oo_ref, ks, vts, sa, sb, ma, mb,
                 *, lambda_init, nstep, tq):
    i = pl.program_id(2)
    t = kl_ref.shape[1]

    @pl.when(i == 0)
    def _():
        ks[0:t, :] = kl_ref[0]
        ks[t:, :] = kc_ref[0]
        vts[0:LANES, 0:t] = vl_ref[0].astype(F32).T.astype(BF16)
        vts[0:LANES, t:] = vc_ref[0].astype(F32).T.astype(BF16)
        vts[LANES:, :] = jnp.ones((vts.shape[0] - LANES, vts.shape[1]), BF16)

    def scores(half, s_scr, m_scr):
        q = q_ref[0, half * tq:(half + 1) * tq, :]
        k = ks[...]
        lane = lax.broadcasted_iota(I32, q.shape, 1)
        zero = jnp.zeros_like(q)
        for h, qh in enumerate((jnp.where(lane < DIFF_HEAD_DIM, q, zero),
                                jnp.where(lane >= DIFF_HEAD_DIM, q, zero))):
            s = lax.dot_general(k, qh, (((1,), (1,)), ((), ())), preferred_element_type=F32)
            s_scr[h] = s
            m_scr[h] = jnp.max(s, axis=0, keepdims=True)

    def finish(s_scr, m_scr, o_ref):
        vt = vts[...]
        lp = lam_ref[...]
        lam = (jnp.exp(jnp.sum(lp[0:1] * lp[1:2], axis=-1, keepdims=True))
               - jnp.exp(jnp.sum(lp[2:3] * lp[3:4], axis=-1, keepdims=True)) + lambda_init)
        outs = []
        for h in range(2):
            p = jnp.exp2(s_scr[h] - m_scr[h]).astype(BF16)
            r = jnp.dot(vt, p, preferred_element_type=F32)
            outs.append(r[0:LANES, :] / r[LANES:LANES + 1, :])
        o = outs[0] - lam * outs[1]
        o = o * lax.rsqrt(jnp.mean(o * o, axis=0, keepdims=True) + EPS)
        o_ref[0] = (o.T * subln_ref[...] * (1.0 - lambda_init)).astype(o_ref.dtype)

    @pl.when(i == 0)
    def _():
        scores(0, sa, ma)

    @pl.when(jnp.logical_and(i > 0, i < nstep))
    def _():
        finish(sb, mb, oo_ref)
        scores(0, sa, ma)

    @pl.when(i < nstep)
    def _():
        finish(sa, ma, oe_ref)
        scores(1, sb, mb)

    @pl.when(i == nstep)
    def _():
        finish(sb, mb, oo_ref)


def _attention(z, zc, lam_params, subln, tq, lambda_init):
    b, t, n3 = z.shape
    tc = zc.shape[1]
    d = n3 // 3
    npair = d // LANES
    nstep = t // (2 * tq)
    s_scr = pltpu.VMEM((2, t + tc, tq), F32)
    m_scr = pltpu.VMEM((2, 1, tq), F32)
    half = jax.ShapeDtypeStruct((b, t // 2, d), BF16)
    return pl.pallas_call(
        functools.partial(_attn_kernel, lambda_init=lambda_init, nstep=nstep, tq=tq),
        grid=(b, npair, nstep + 1),
        in_specs=[pl.BlockSpec((1, 2 * tq, LANES), lambda bb, p, i: (bb, jnp.minimum(i, nstep - 1), p)),
                  pl.BlockSpec((1, t, LANES), lambda bb, p, i: (bb, 0, npair + p)),
                  pl.BlockSpec((1, t, LANES), lambda bb, p, i: (bb, 0, 2 * npair + p)),
                  pl.BlockSpec((1, tc, LANES), lambda bb, p, i: (bb, 0, npair + p)),
                  pl.BlockSpec((1, tc, LANES), lambda bb, p, i: (bb, 0, 2 * npair + p)),
                  pl.BlockSpec(lam_params.shape, lambda bb, p, i: (0, 0)),
                  pl.BlockSpec((1, LANES), lambda bb, p, i: (0, 0))],
        out_specs=[pl.BlockSpec((1, tq, LANES), lambda bb, p, i: (bb, jnp.minimum(i, nstep - 1), p)),
                   pl.BlockSpec((1, tq, LANES), lambda bb, p, i: (bb, jnp.maximum(i - 1, 0), p))],
        out_shape=[half, half],
        scratch_shapes=[pltpu.VMEM((t + tc, LANES), BF16), pltpu.VMEM((LANES + 2 * SUBLANES, t + tc), BF16),
                        s_scr, s_scr, m_scr, m_scr],
        compiler_params=_cparams(("arbitrary", "arbitrary", "arbitrary")),
        name="attention",
    )(z, z, z, zc, zc, lam_params.astype(F32), subln.reshape(1, LANES).astype(F32))


def _moe_kernel(te_ref, tv_ref, pdst_ref, src_ref, nsrc_ref, wg_ref, wu_ref, wd_ref, h_hbm,
                y_hbm, gbuf, hbuf, acc, obuf, gsem, ssem, *, tm):
    del te_ref
    j = pl.program_id(0)
    f = pl.program_id(1)
    nf = pl.num_programs(1)
    slot = j % 2
    other = 1 - slot
    rows_per_step = tm // MOE_HIDDEN_STEPS
    valid = tv_ref[j] > 0
    drain = jnp.logical_and(jnp.logical_not(valid), tv_ref[jnp.maximum(j - 1, 0)] * (j > 0).astype(I32) > 0)

    def gather_row(idx_ref, to_slot, g, k):
        tok = idx_ref[0, 0, g * SUBLANES + k]
        pltpu.make_async_copy(h_hbm.at[tok], gbuf.at[to_slot, :, g, pl.ds(k, 1), :], gsem.at[to_slot]).start()

    def scatter_row(idx_ref, from_slot, g, k):
        row = idx_ref[0, 0, g * SUBLANES + k]
        pltpu.make_async_copy(obuf.at[from_slot, :, g, pl.ds(k, 1), :], y_hbm.at[:, pl.ds(row, 1), :],
                              ssem.at[from_slot]).start()

    def loop_rows(fn):
        def body(g, c):
            for k in range(SUBLANES):
                fn(g, k)
            return c
        lax.fori_loop(0, tm // SUBLANES, body, 0)

    def wait_gather(s):
        pltpu.make_async_copy(gbuf.at[1 - s], gbuf.at[s], gsem.at[s]).wait()

    def wait_scatter(s):
        pltpu.make_async_copy(obuf.at[1 - s], obuf.at[s], ssem.at[s]).wait()

    @pl.when(jnp.logical_and(j == 0, f == 0))
    def _():
        obuf[1] = jnp.zeros(obuf.shape[1:], U32)
        loop_rows(lambda g, k: gather_row(src_ref, 0, g, k))

    @pl.when(jnp.logical_and(jnp.logical_or(valid, drain), f == 0))
    def _():
        wait_gather(slot)

    @pl.when(jnp.logical_and(valid, f == 0))
    def _():
        n_slab = gbuf.shape[1]
        for sl in range(n_slab):
            lo, hi = _unpack_rows(gbuf[slot, sl].reshape(tm, LANES))
            hbuf[:, sl * LANES:(sl + 1) * LANES] = lo.astype(BF16)
            hbuf[:, (n_slab + sl) * LANES:(n_slab + sl + 1) * LANES] = hi.astype(BF16)

    @pl.when(valid)
    def _():
        base = f * (rows_per_step // SUBLANES)
        for k in range(rows_per_step):
            gather_row(nsrc_ref, other, base + k // SUBLANES, k % SUBLANES)
            scatter_row(pdst_ref, other, base + k // SUBLANES, k % SUBLANES)
        h = hbuf[...]
        g = jnp.dot(h, wg_ref[0].astype(BF16), preferred_element_type=F32)
        u = jnp.dot(h, wu_ref[0].astype(BF16), preferred_element_type=F32)
        a = (_silu(g) * u).astype(BF16)
        part = jnp.dot(a, wd_ref[0].astype(BF16), preferred_element_type=F32)

        @pl.when(f == 0)
        def _():
            acc[...] = part

        @pl.when(f > 0)
        def _():
            acc[...] += part

    @pl.when(jnp.logical_and(valid, f == nf - 1))
    def _():
        @pl.when(j > 0)
        def _():
            wait_scatter(slot)

        packed = _pack_rows(acc[...])
        for sl in range(obuf.shape[1]):
            obuf[slot, sl] = packed[:, sl * LANES:(sl + 1) * LANES].reshape(tm // SUBLANES, SUBLANES, LANES)

    @pl.when(jnp.logical_and(drain, f == 0))
    def _():
        loop_rows(lambda g, k: scatter_row(pdst_ref, other, g, k))
        wait_scatter(other)
        wait_scatter(slot)


def _moe(hp, te, tv, src, dst, w_gu, w_down, tm):
    nt = te.shape[0]
    n_tok, n_slab = hp.shape[:2]
    e, d, two_ff = w_gu.shape
    dff = two_ff // 2
    nf = MOE_HIDDEN_STEPS
    tf = dff // nf

    def wf(j, f, te_r, tv_r):
        return jnp.where(tv_r[j] > 0, f, nf - 1)

    def inv_spec(off):
        return pl.BlockSpec((1, 1, tm), lambda j, f, a, c: (jnp.minimum(j + off, nt), 0, 0),
                            memory_space=pltpu.SMEM)

    grid_spec = pltpu.PrefetchScalarGridSpec(
        num_scalar_prefetch=2,
        grid=(nt, nf),
        in_specs=[inv_spec(0), inv_spec(1), inv_spec(2),
                  pl.BlockSpec((1, d, tf), lambda j, f, a, c: (a[j], 0, wf(j, f, a, c))),
                  pl.BlockSpec((1, d, tf), lambda j, f, a, c: (a[j], 0, wf(j, f, a, c) + nf)),
                  pl.BlockSpec((1, tf, d), lambda j, f, a, c: (a[j], wf(j, f, a, c), 0)),
                  pl.BlockSpec(memory_space=pl.ANY)],
        out_specs=pl.BlockSpec(memory_space=pl.ANY),
        scratch_shapes=[pltpu.VMEM((2, n_slab, tm // SUBLANES, SUBLANES, LANES), U32),
                        pltpu.VMEM((tm, d), BF16),
                        pltpu.VMEM((tm, d), F32),
                        pltpu.VMEM((2, n_slab, tm // SUBLANES, SUBLANES, LANES), U32),
                        pltpu.SemaphoreType.DMA((2,)),
                        pltpu.SemaphoreType.DMA((2,))],
    )
    return pl.pallas_call(
        functools.partial(_moe_kernel, tm=tm),
        grid_spec=grid_spec,
        out_shape=jax.ShapeDtypeStruct((n_slab, 2 * n_tok + tm, LANES), U32),
        compiler_params=_cparams(("arbitrary", "arbitrary")),
        name="moe",
    )(te, tv, dst, src, src, w_gu, w_gu, w_down, hp)


def _route_plan(rt, tm, nt):
    t = rt.shape[0]
    e_flat = rt[:, 0:2].astype(I32).reshape(-1)
    onehot = (e_flat[:, None] == jnp.arange(N_EXPERTS, dtype=I32)[None, :]).astype(I32)
    csum = jnp.cumsum(onehot, axis=0)
    rank = jnp.sum(csum * onehot, axis=1) - 1
    counts = csum[-1]
    tiles_e = (counts + tm - 1) // tm
    tile_end = jnp.cumsum(tiles_e)
    row_off = (tile_end - tiles_e) * tm
    dest = jnp.sum(row_off[None, :] * onehot, axis=1) + rank
    inv = jnp.full((nt * tm,), 2 * t, I32).at[dest].set(jnp.arange(2 * t, dtype=I32))
    n_used = tile_end[-1]
    tid = jnp.arange(nt, dtype=I32)
    tv = (tid < n_used).astype(I32)
    te = jnp.sum((jnp.minimum(tid, n_used - 1)[:, None] >= tile_end[None, :]).astype(I32), axis=1)
    inv = jnp.concatenate([jnp.full((tm,), 2 * t, I32), inv])
    pad_row = 2 * t + jnp.arange(inv.shape[0], dtype=I32) % tm
    src = jnp.minimum(inv >> 1, t - 1)
    dst = jnp.where(inv < 2 * t, (inv & 1) * t + (inv >> 1), pad_row)
    return jnp.minimum(te, N_EXPERTS - 1), tv, src.reshape(nt + 1, 1, tm), dst.reshape(nt + 1, 1, tm)


def _combine_kernel(y0_ref, y1_ref, rt_ref, x_ref, gpost_ref, mod_ref, o_ref):
    w0 = rt_ref[0, :, 2:3]
    w1 = rt_ref[0, :, 3:4]
    los, his = [], []
    for sl in range(y0_ref.shape[0]):
        lo0, hi0 = _unpack_rows(y0_ref[sl])
        lo1, hi1 = _unpack_rows(y1_ref[sl])
        los.append(w0 * lo0 + w1 * lo1)
        his.append(w0 * hi0 + w1 * hi1)
    y = jnp.concatenate(los + his, axis=1)
    o_ref[0] = x_ref[0] + mod_ref[0, G2:G2 + 1, :] * _rms_rows(y, gpost_ref[...])


def _combine(y2, rt, x, gpost, mod, tm):
    b, t, d = x.shape
    per_b = t // tm
    nblk = b * per_b
    return pl.pallas_call(
        _combine_kernel,
        grid=(b, per_b),
        in_specs=[pl.BlockSpec((d // 2 // LANES, tm, LANES), lambda bb, i: (0, bb * per_b + i, 0)),
                  pl.BlockSpec((d // 2 // LANES, tm, LANES), lambda bb, i: (0, nblk + bb * per_b + i, 0)),
                  pl.BlockSpec((1, tm, LANES), lambda bb, i: (bb, i, 0)),
                  pl.BlockSpec((1, tm, d), lambda bb, i: (bb, i, 0)),
                  pl.BlockSpec((1, d), lambda bb, i: (0, 0)),
                  pl.BlockSpec((1, 6, d), lambda bb, i: (bb, 0, 0))],
        out_specs=pl.BlockSpec((1, tm, d), lambda bb, i: (bb, i, 0)),
        out_shape=jax.ShapeDtypeStruct((b, t, d), F32),
        compiler_params=_cparams(("arbitrary", "arbitrary")),
        name="combine",
    )(y2, y2, rt, x, gpost.reshape(1, d), mod)


def _rope_angles(rows, head_dim):
    pos_r = jnp.repeat(jnp.arange(rows, dtype=F32), GRID_W)
    pos_c = jnp.tile(jnp.arange(GRID_W, dtype=F32), rows)
    axis_dim = head_dim // 2
    freqs = ROPE_BASE ** (-jnp.arange(0, axis_dim, 2, dtype=F32) / axis_dim)
    return pos_r[:, None] * freqs[None, :], pos_c[:, None] * freqs[None, :]


def _rope_tables(rows, head_dim, reps):
    ar, ac = _rope_angles(rows, head_dim)
    cos = jnp.concatenate([jnp.cos(ar), jnp.cos(ar), jnp.cos(ac), jnp.cos(ac)], axis=1)
    zr = jnp.zeros_like(ar)
    s_up = jnp.concatenate([-jnp.sin(ar), zr, -jnp.sin(ac), zr], axis=1)
    s_dn = jnp.concatenate([zr, jnp.sin(ar), zr, jnp.sin(ac)], axis=1)
    return jnp.stack([jnp.tile(cos, (1, reps)), jnp.tile(s_up, (1, reps)), jnp.tile(s_dn, (1, reps))])


def _identity_tables(n_tab, tm, period):
    return jnp.concatenate([jnp.ones((1, tm, period), F32), jnp.zeros((n_tab - 1, tm, period), F32)])


def _tile(t, want):
    return min(t, want)


def kernel(x, c, ctx, c_ctx, l0_w_mod, l0_b_mod, l0_norm_pre_mix, l0_norm_post_mix, l0_norm_pre_ffn, l0_norm_post_ffn, l0_ret_w_in, l0_ret_decay_logit, l0_ret_w_out, l0_ffn_w_gate_up, l0_ffn_w_down, l1_w_mod, l1_b_mod, l1_norm_pre_mix, l1_norm_post_mix, l1_norm_pre_ffn, l1_norm_post_ffn, l1_attn_w_in, l1_attn_lambda, l1_attn_subln, l1_attn_w_out, l1_router_w, l1_router_b, l1_moe_w_gate_up, l1_moe_w_down):
    b, t, d = x.shape
    tc = ctx.shape[1]
    rows = t // GRID_W
    assert t % tc == 0

    n_rows = -(-(b + 1) // 8) * 8
    cc = jnp.concatenate([c, c_ctx[None, :], jnp.zeros((n_rows - b - 1, d), F32)], axis=0)
    lat_row = lambda bb: bb
    ctx_row = lambda bb: b

    tm = 2 * OUTPROJ_ROW_GROUP
    tn = 1024
    assert t % tm == 0

    mod0 = _modulation(cc, l0_w_mod, l0_b_mod)
    w_in0 = l0_ret_w_in.astype(BF16)
    dk = w_in0.shape[1] // (8 * RET_HEADS)
    tab_ret = _rope_tables(rows, dk, 1)
    tab_ret = jnp.stack([tab_ret[0], tab_ret[1] + tab_ret[2]])
    ret_scales = (1.0,) * (RET_HEADS * dk // tn) + (dk ** -0.5,) * (RET_HEADS * dk // tn)
    z0 = _inproj(x, l0_norm_pre_mix, mod0, lat_row, w_in0, tab_ret, True, tm, tn, ret_scales, (dk // 4,))
    zc0 = _inproj(ctx, l0_norm_pre_mix, mod0, ctx_row, w_in0, _identity_tables(2, tc, dk), False, tc, tn,
                  ret_scales, (dk // 4,))
    yf, yb, yfc, ybc = _retention(z0, zc0, l0_ret_decay_logit)

    w_out0 = l0_ret_w_out.astype(BF16)
    x1, h1 = _outproj([yf, yb], w_out0, x, l0_norm_post_mix, l0_norm_pre_ffn, mod0, lat_row, tm)
    c1, hc1 = _outproj([yfc, ybc], w_out0, ctx, l0_norm_post_mix, l0_norm_pre_ffn, mod0, ctx_row, tc)

    w_gu0 = l0_ffn_w_gate_up.astype(BF16)
    w_dn0 = l0_ffn_w_down.astype(BF16)
    x2 = _ffn(h1, w_gu0, w_dn0, x1, l0_norm_post_ffn, mod0, lat_row, tm, FFN_HIDDEN_TILE)
    c2 = _ffn(hc1, w_gu0, w_dn0, c1, l0_norm_post_ffn, mod0, ctx_row, tc, FFN_HIDDEN_TILE)

    lambda_init = 0.8 - 0.6 * math.exp(-0.3 * 1)
    mod1 = _modulation(cc, l1_w_mod, l1_b_mod)
    w_in1 = l1_attn_w_in.astype(BF16)
    tab_diff = _rope_tables(rows, DIFF_HEAD_DIM, LANES // DIFF_HEAD_DIM)
    q4 = DIFF_HEAD_DIM // 4
    diff_scales = (DIFF_HEAD_DIM ** -0.5 * LOG2_E,) * (d // tn) + (1.0,) * (d // tn)
    z1 = _inproj(x2, l1_norm_pre_mix, mod1, lat_row, w_in1, tab_diff, True, tm, tn, diff_scales,
                 (LANES - q4, q4))
    zc1 = _inproj(c2, l1_norm_pre_mix, mod1, ctx_row, w_in1, _identity_tables(3, tc, LANES), False, tc, tn,
                  (1.0,) * (2 * d // tn), (LANES - q4, q4))
    o_even, o_odd = _attention(z1, zc1, l1_attn_lambda, l1_attn_subln, OUTPROJ_ROW_GROUP, lambda_init)

    rw = jnp.zeros((d, LANES), F32).at[:, :N_EXPERTS].set(l1_router_w)
    rb = jnp.full((1, LANES), NEG_BIG, F32).at[0, :N_EXPERTS].set(l1_router_b)
    x3, hp, rt = _outproj([o_even, o_odd], l1_attn_w_out.astype(BF16), x2, l1_norm_post_mix, l1_norm_pre_ffn,
                          mod1, lat_row, tm, router=(rw, rb), y_per_group=True)

    n_tok = b * t
    tm_e = _tile(n_tok, 1024)
    nt = 2 * n_tok // tm_e + N_EXPERTS
    te, tv, src, dst = _route_plan(rt.reshape(n_tok, LANES), tm_e, nt)
    y2 = _moe(hp.reshape(n_tok, d // 2 // LANES, 1, LANES), te, tv, src, dst, l1_moe_w_gate_up,
              l1_moe_w_down, tm_e)
    return _combine(y2, rt, x3, l1_norm_post_ffn, mod1, tm)
```

```python
import functools
import math

import jax
import jax.numpy as jnp
from jax import lax
from jax.experimental import pallas as pl
from jax.experimental.pallas import tpu as pltpu

F32 = jnp.float32
BF16 = jnp.bfloat16
U32 = jnp.uint32
I32 = jnp.int32

EPS = 1e-6
GRID_W = 64
ROPE_BASE = 10000.0
RET_HEADS = 4
DIFF_HEADS = 8
DIFF_HEAD_DIM = 64
N_EXPERTS = 8
LANES = 128
SUBLANES = 8
NEG_BIG = -1e30
VMEM_LIMIT = 56 * 1024 * 1024
DMA_UNROLL = 8
MOE_HIDDEN_STEPS = 4
FFN_HIDDEN_TILE = 256
RET_CHUNKS_PER_STEP = 4
OUTPROJ_ROW_GROUP = 256
LOG2_E = 1.4426950408889634

SH1, SC1, G1, SH2, SC2, G2 = range(6)


def _cparams(sem):
    return pltpu.CompilerParams(dimension_semantics=sem, vmem_limit_bytes=VMEM_LIMIT)


def _rms_rows(x, gain):
    return x * lax.rsqrt(jnp.mean(x * x, axis=-1, keepdims=True) + EPS) * gain


def _silu(x):
    return x * jax.nn.sigmoid(x)


def _mod_kernel(c_ref, w_ref, b_ref, o_ref):
    a = _silu(c_ref[...])
    o_ref[...] = jnp.dot(a, w_ref[...], preferred_element_type=F32,
                         precision=lax.Precision.HIGHEST) + b_ref[...]


def _modulation(cc, w_mod, b_mod):
    r, d = cc.shape
    n = w_mod.shape[1]
    tn = n // 6
    out = pl.pallas_call(
        _mod_kernel,
        grid=(n // tn,),
        in_specs=[pl.BlockSpec((r, d), lambda j: (0, 0)),
                  pl.BlockSpec((d, tn), lambda j: (0, j)),
                  pl.BlockSpec((1, tn), lambda j: (0, j))],
        out_specs=pl.BlockSpec((r, tn), lambda j: (0, j)),
        out_shape=jax.ShapeDtypeStruct((r, n), F32),
        compiler_params=_cparams(("arbitrary",)),
        name="mod",
    )(cc, w_mod, b_mod.reshape(1, n))
    return out.reshape(r, 6, d)


def _inproj_kernel(x_ref, g_ref, mod_ref, w_ref, tab_ref, o_ref, *, tn, rope_scales, shifts, period):
    y = _rms_rows(x_ref[0], g_ref[...])
    h = (y * (1.0 + mod_ref[0, SC1:SC1 + 1, :]) + mod_ref[0, SH1:SH1 + 1, :]).astype(BF16)
    for j in range(w_ref.shape[1] // tn):
        acc = jnp.dot(h, w_ref[:, j * tn:(j + 1) * tn], preferred_element_type=F32)
        if j >= len(rope_scales):
            o_ref[0, :, j * tn:(j + 1) * tn] = acc.astype(o_ref.dtype)
            continue
        for k in range(tn // LANES):
            xs = acc[:, k * LANES:(k + 1) * LANES]
            p = (k * LANES) % period
            r = xs * tab_ref[0, :, p:p + LANES]
            for si, sft in enumerate(shifts):
                r = r + pltpu.roll(xs, sft, 1) * tab_ref[1 + si, :, p:p + LANES]
            if rope_scales[j] != 1.0:
                r = r * rope_scales[j]
            c0 = j * tn + k * LANES
            o_ref[0, :, c0:c0 + LANES] = r.astype(o_ref.dtype)


def _inproj(x, gain, mod, mod_row, w, tab, tab_per_tile, tm, tn, rope_scales, shifts):
    b, t, d = x.shape
    n = w.shape[1]
    period = tab.shape[2]
    kern = functools.partial(_inproj_kernel, tn=tn, rope_scales=rope_scales, shifts=shifts, period=period)
    tab_map = (lambda bb, i: (0, i, 0)) if tab_per_tile else (lambda bb, i: (0, 0, 0))
    return pl.pallas_call(
        kern,
        grid=(b, t // tm),
        in_specs=[pl.BlockSpec((1, tm, d), lambda bb, i: (bb, i, 0)),
                  pl.BlockSpec((1, d), lambda bb, i: (0, 0)),
                  pl.BlockSpec((1, 6, d), lambda bb, i: (mod_row(bb), 0, 0)),
                  pl.BlockSpec((d, n), lambda bb, i: (0, 0), pipeline_mode=pl.Buffered(1)),
                  pl.BlockSpec((tab.shape[0], tm, period), tab_map)],
        out_specs=pl.BlockSpec((1, tm, n), lambda bb, i: (bb, i, 0)),
        out_shape=jax.ShapeDtypeStruct((b, t, n), BF16),
        compiler_params=_cparams(("arbitrary", "arbitrary")),
        name="inproj",
    )(x, gain.reshape(1, d), mod, w, tab)


def _ret_kernel(qf, kf, vf, gf, qb, kb, vb, gb, qc, kc, vc, gfc, gbc, intra, qdec, kdec, cdec,
                yf, yb, yfc, ybc, st):
    s = pl.program_id(2)
    chunk = qc.shape[1]

    def direction(d, q_ref, k_ref, v_ref, g_ref, y_ref, sub):
        rs = slice(sub * chunk, (sub + 1) * chunk)
        q = q_ref[0, rs, :]
        k = k_ref[0, rs, :]
        v = v_ref[0, rs, :]
        scores = lax.dot_general(q, k, (((1,), (1,)), ((), ())), preferred_element_type=F32) * intra[d, 0]
        state = st[d]
        o = (jnp.dot(scores.astype(BF16), v, preferred_element_type=F32)
             + qdec[d, 0] * jnp.dot(q, state.astype(BF16), preferred_element_type=F32))
        ks = (k.astype(F32) * kdec[d, 0]).astype(BF16)
        st[d] = state * cdec[d, 0] + lax.dot_general(ks, v, (((0,), (0,)), ((), ())),
                                                     preferred_element_type=F32)
        hn = o * lax.rsqrt(jnp.mean(o * o, axis=-1, keepdims=True) + EPS)
        y_ref[0, rs, :] = (_silu(g_ref[0, rs, :].astype(F32)) * hn).astype(y_ref.dtype)

    @pl.when(s == 0)
    def _():
        st[...] = jnp.zeros(st.shape, F32)
        direction(0, qc, kc, vc, gfc, yfc, 0)
        direction(1, qc, kc, vc, gbc, ybc, 0)

    @pl.when(s > 0)
    def _():
        for c in range(RET_CHUNKS_PER_STEP):
            direction(0, qf, kf, vf, gf, yf, c)
            direction(1, qb, kb, vb, gb, yb, RET_CHUNKS_PER_STEP - 1 - c)


def _retention(z, zc, decay_logit):
    b, t, _ = z.shape
    chunk = zc.shape[1]
    n_lat = t // chunk
    h = RET_HEADS
    dk = z.shape[2] // (8 * h)
    dv = 2 * dk

    log_g = jax.nn.log_sigmoid(decay_logit.astype(F32))
    idx = jnp.arange(chunk, dtype=F32)
    diff = idx[:, None] - idx[None, :]
    lg = log_g[:, :, None, None]
    intra_f = jnp.where(diff >= 0, jnp.exp(lg[0] * jnp.maximum(diff, 0.0)), 0.0)
    intra_b = jnp.where(diff <= 0, jnp.exp(lg[1] * jnp.maximum(-diff, 0.0)), 0.0)
    intra = jnp.stack([intra_f, intra_b])
    qdec = jnp.stack([jnp.exp(log_g[0][:, None] * (idx + 1.0)),
                      jnp.exp(log_g[1][:, None] * (chunk - idx))])[..., None]
    kdec = jnp.stack([jnp.exp(log_g[0][:, None] * (chunk - 1.0 - idx)),
                      jnp.exp(log_g[1][:, None] * idx)])[..., None]
    cdec = jnp.exp(log_g * chunk)[..., None, None]

    cps = RET_CHUNKS_PER_STEP
    assert n_lat % cps == 0
    n_blk = n_lat // cps

    def fwd(s):
        return jnp.maximum(s - 1, 0)

    def bwd(s):
        return n_blk - jnp.maximum(s, 1)

    def spec(width, col0, chunk_of):
        return pl.BlockSpec((1, cps * chunk, width), lambda bb, hh, s: (bb, chunk_of(s), col0 + hh))

    def cspec(width, col0):
        return pl.BlockSpec((1, chunk, width), lambda bb, hh, s: (bb, 0, col0 + hh))

    in_specs = [spec(dk, 0, fwd), spec(dk, h, fwd), spec(dv, h, fwd), spec(dv, 2 * h, fwd),
                spec(dk, 0, bwd), spec(dk, h, bwd), spec(dv, h, bwd), spec(dv, 3 * h, bwd),
                cspec(dk, 0), cspec(dk, h), cspec(dv, h), cspec(dv, 2 * h), cspec(dv, 3 * h),
                pl.BlockSpec((2, 1, chunk, chunk), lambda bb, hh, s: (0, hh, 0, 0)),
                pl.BlockSpec((2, 1, chunk, 1), lambda bb, hh, s: (0, hh, 0, 0)),
                pl.BlockSpec((2, 1, chunk, 1), lambda bb, hh, s: (0, hh, 0, 0)),
                pl.BlockSpec((2, 1, 1, 1), lambda bb, hh, s: (0, hh, 0, 0))]
    out_specs = [pl.BlockSpec((1, cps * chunk, dv), lambda bb, hh, s: (bb, fwd(s), hh)),
                 pl.BlockSpec((1, cps * chunk, dv), lambda bb, hh, s: (bb, bwd(s), hh)),
                 pl.BlockSpec((1, chunk, dv), lambda bb, hh, s: (bb, 0, hh)),
                 pl.BlockSpec((1, chunk, dv), lambda bb, hh, s: (bb, 0, hh))]
    yl = jax.ShapeDtypeStruct((b, t, h * dv), BF16)
    yc = jax.ShapeDtypeStruct((b, chunk, h * dv), BF16)
    return pl.pallas_call(
        _ret_kernel,
        grid=(b, h, n_blk + 1),
        in_specs=in_specs,
        out_specs=out_specs,
        out_shape=[yl, yl, yc, yc],
        scratch_shapes=[pltpu.VMEM((2, dk, dv), F32)],
        compiler_params=_cparams(("arbitrary", "arbitrary", "arbitrary")),
        name="retention",
    )(z, z, z, z, z, z, z, z, zc, zc, zc, zc, zc, intra, qdec, kdec, cdec)


def _pack_rows(h):
    n = h.shape[1] // 2
    bits = lax.bitcast_convert_type(h.astype(BF16).astype(F32), U32)
    return (bits[:, :n] >> 16) | (bits[:, n:] & jnp.uint32(0xFFFF0000))


def _unpack_rows(w):
    lo = lax.bitcast_convert_type(w << 16, F32)
    hi = lax.bitcast_convert_type(w & jnp.uint32(0xFFFF0000), F32)
    return lo, hi


def _outproj_kernel(*refs, n_y, route, y_per_group):
    y_refs = refs[:n_y]
    w_ref, x_ref, gpost_ref, gpre_ref, mod_ref = refs[n_y:n_y + 5]
    rest = refs[n_y + 5:]
    if route:
        rw_ref, rb_ref, xo_ref, hp_ref, rt_ref = rest
    else:
        xo_ref, h_ref = rest
    tm = x_ref.shape[1]
    rows = min(tm, OUTPROJ_ROW_GROUP)
    for c in range(tm // rows):
        rs = slice(c * rows, (c + 1) * rows)
        if y_per_group:
            y = y_refs[c][0]
        else:
            y = y_refs[0][0, rs, :]
            if n_y == 2:
                y = (y.astype(F32) + y_refs[1][0, rs, :].astype(F32)).astype(BF16)
        o = jnp.dot(y, w_ref[...], preferred_element_type=F32)
        xn = x_ref[0, rs, :] + mod_ref[0, G1:G1 + 1, :] * _rms_rows(o, gpost_ref[...])
        xo_ref[0, rs, :] = xn
        h = _rms_rows(xn, gpre_ref[...]) * (1.0 + mod_ref[0, SC2:SC2 + 1, :]) + mod_ref[0, SH2:SH2 + 1, :]
        if not route:
            h_ref[0, rs, :] = h.astype(h_ref.dtype)
            continue
        packed = _pack_rows(h)
        for sl in range(packed.shape[1] // LANES):
            hp_ref[0, rs, sl, 0, :] = packed[:, sl * LANES:(sl + 1) * LANES]
        logits = jnp.dot(h, rw_ref[...], preferred_element_type=F32,
                         precision=lax.Precision.HIGHEST) + rb_ref[...]
        lane = lax.broadcasted_iota(I32, logits.shape, 1).astype(F32)
        m1 = jnp.max(logits, axis=-1, keepdims=True)
        i1 = jnp.min(jnp.where(logits == m1, lane, float(LANES)), axis=-1, keepdims=True)
        masked = jnp.where(lane == i1, NEG_BIG, logits)
        m2 = jnp.max(masked, axis=-1, keepdims=True)
        i2 = jnp.min(jnp.where(masked == m2, lane, float(LANES)), axis=-1, keepdims=True)
        e2 = jnp.exp(m2 - m1)
        den = 1.0 + e2
        w1 = 1.0 / den
        w2 = e2 / den
        rt_ref[0, rs, :] = jnp.where(lane == 0.0, i1,
                                     jnp.where(lane == 1.0, i2,
                                               jnp.where(lane == 2.0, w1, jnp.where(lane == 3.0, w2, 0.0))))


def _outproj(ys, w, x, gpost, gpre, mod, mod_row, tm, router=None, y_per_group=False):
    b, t, d = x.shape
    kdim = w.shape[0]
    route = router is not None
    ytm = min(tm, OUTPROJ_ROW_GROUP) if y_per_group else tm
    in_specs = [pl.BlockSpec((1, ytm, kdim), lambda bb, i: (bb, i, 0)) for _ in ys]
    in_specs += [pl.BlockSpec((kdim, d), lambda bb, i: (0, 0)),
                 pl.BlockSpec((1, tm, d), lambda bb, i: (bb, i, 0)),
                 pl.BlockSpec((1, d), lambda bb, i: (0, 0)),
                 pl.BlockSpec((1, d), lambda bb, i: (0, 0)),
                 pl.BlockSpec((1, 6, d), lambda bb, i: (mod_row(bb), 0, 0))]
    args = list(ys) + [w, x, gpost.reshape(1, d), gpre.reshape(1, d), mod]
    out_specs = [pl.BlockSpec((1, tm, d), lambda bb, i: (bb, i, 0))]
    out_shape = [jax.ShapeDtypeStruct((b, t, d), F32)]
    if route:
        rw, rb = router
        in_specs += [pl.BlockSpec((d, LANES), lambda bb, i: (0, 0)),
                     pl.BlockSpec((1, LANES), lambda bb, i: (0, 0))]
        args += [rw, rb]
        out_specs += [pl.BlockSpec((1, tm, d // 2 // LANES, 1, LANES), lambda bb, i: (bb, i, 0, 0, 0)),
                      pl.BlockSpec((1, tm, LANES), lambda bb, i: (bb, i, 0))]
        out_shape += [jax.ShapeDtypeStruct((b, t, d // 2 // LANES, 1, LANES), U32),
                      jax.ShapeDtypeStruct((b, t, LANES), F32)]
    else:
        out_specs.append(pl.BlockSpec((1, tm, d), lambda bb, i: (bb, i, 0)))
        out_shape.append(jax.ShapeDtypeStruct((b, t, d), BF16))
    return pl.pallas_call(
        functools.partial(_outproj_kernel, n_y=len(ys), route=route, y_per_group=y_per_group),
        grid=(b, t // tm),
        in_specs=in_specs,
        out_specs=out_specs,
        out_shape=out_shape,
        compiler_params=_cparams(("arbitrary", "arbitrary")),
        name="outproj",
    )(*args)


def _ffn_kernel(h_ref, wgu_ref, wd_ref, x_ref, gpost_ref, mod_ref, o_ref, *, tf):
    h = h_ref[0]
    dff = wd_ref.shape[0]
    acc = None
    for c in range(dff // tf):
        g = jnp.dot(h, wgu_ref[:, c * tf:(c + 1) * tf], preferred_element_type=F32)
        u = jnp.dot(h, wgu_ref[:, dff + c * tf:dff + (c + 1) * tf], preferred_element_type=F32)
        a = (_silu(g) * u).astype(BF16)
        part = jnp.dot(a, wd_ref[c * tf:(c + 1) * tf, :], preferred_element_type=F32)
        acc = part if acc is None else acc + part
    o_ref[0] = x_ref[0] + mod_ref[0, G2:G2 + 1, :] * _rms_rows(acc, gpost_ref[...])


def _ffn(h, w_gu, w_down, x, gpost, mod, mod_row, tm, tf):
    b, t, d = x.shape
    dff = w_down.shape[0]
    return pl.pallas_call(
        functools.partial(_ffn_kernel, tf=tf),
        grid=(b, t // tm),
        in_specs=[pl.BlockSpec((1, tm, d), lambda bb, i: (bb, i, 0)),
                  pl.BlockSpec((d, 2 * dff), lambda bb, i: (0, 0), pipeline_mode=pl.Buffered(1)),
                  pl.BlockSpec((dff, d), lambda bb, i: (0, 0), pipeline_mode=pl.Buffered(1)),
                  pl.BlockSpec((1, tm, d), lambda bb, i: (bb, i, 0)),
                  pl.BlockSpec((1, d), lambda bb, i: (0, 0)),
                  pl.BlockSpec((1, 6, d), lambda bb, i: (mod_row(bb), 0, 0))],
        out_specs=pl.BlockSpec((1, tm, d), lambda bb, i: (bb, i, 0)),
        out_shape=jax.ShapeDtypeStruct((b, t, d), F32),
        compiler_params=_cparams(("arbitrary", "arbitrary")),
        name="ffn",
    )(h, w_gu, w_down, x, gpost.reshape(1, d), mod)


def _attn_kernel(q_ref, kl_ref, vl_ref, kc_ref, vc_ref, lam_ref, subln_ref, oe_ref, oo_ref, ks, vs, sa, sb, ma, mb,
                 *, lambda_init, nstep, tq):
    i = pl.program_id(2)
    t = kl_ref.shape[1]

    @pl.when(i == 0)
    def _():
        ks[0:t, :] = kl_ref[0]
        ks[t:, :] = kc_ref[0]
        vs[0:t, 0:LANES] = vl_ref[0]
        vs[t:, 0:LANES] = vc_ref[0]
        vs[:, LANES:] = jnp.ones((vs.shape[0], LANES), BF16)

    def scores(half, s_scr, m_scr):
        q = q_ref[0, half * tq:(half + 1) * tq, :]
        k = ks[...]
        lane = lax.broadcasted_iota(I32, q.shape, 1)
        zero = jnp.zeros_like(q)
        for h, qh in enumerate((jnp.where(lane < DIFF_HEAD_DIM, q, zero),
                                jnp.where(lane >= DIFF_HEAD_DIM, q, zero))):
            s = lax.dot_general(qh, k, (((1,), (1,)), ((), ())), preferred_element_type=F32)
            s_scr[h] = s
            m_scr[h] = jnp.max(s, axis=-1, keepdims=True)

    def finish(s_scr, m_scr, o_ref):
        v = vs[...]
        lp = lam_ref[...]
        lam = (jnp.exp(jnp.sum(lp[0:1] * lp[1:2], axis=-1, keepdims=True))
               - jnp.exp(jnp.sum(lp[2:3] * lp[3:4], axis=-1, keepdims=True)) + lambda_init)
        outs = []
        for h in range(2):
            p = jnp.exp2(s_scr[h] - m_scr[h]).astype(BF16)
            r = jnp.dot(p, v, preferred_element_type=F32)
            outs.append(r[:, 0:LANES] / r[:, LANES:LANES + 1])
        o = outs[0] - lam * outs[1]
        o_ref[0] = (_rms_rows(o, subln_ref[...]) * (1.0 - lambda_init)).astype(o_ref.dtype)

    @pl.when(i == 0)
    def _():
        scores(0, sa, ma)

    @pl.when(jnp.logical_and(i > 0, i < nstep))
    def _():
        scores(0, sa, ma)
        finish(sb, mb, oo_ref)

    @pl.when(i < nstep)
    def _():
        scores(1, sb, mb)
        finish(sa, ma, oe_ref)

    @pl.when(i == nstep)
    def _():
        finish(sb, mb, oo_ref)


def _attention(z, zc, lam_params, subln, tq, lambda_init):
    b, t, n3 = z.shape
    tc = zc.shape[1]
    d = n3 // 3
    npair = d // LANES
    nstep = t // (2 * tq)
    s_scr = pltpu.VMEM((2, tq, t + tc), F32)
    m_scr = pltpu.VMEM((2, tq, 1), F32)
    half = jax.ShapeDtypeStruct((b, t // 2, d), BF16)
    return pl.pallas_call(
        functools.partial(_attn_kernel, lambda_init=lambda_init, nstep=nstep, tq=tq),
        grid=(b, npair, nstep + 1),
        in_specs=[pl.BlockSpec((1, 2 * tq, LANES), lambda bb, p, i: (bb, jnp.minimum(i, nstep - 1), p)),
                  pl.BlockSpec((1, t, LANES), lambda bb, p, i: (bb, 0, npair + p)),
                  pl.BlockSpec((1, t, LANES), lambda bb, p, i: (bb, 0, 2 * npair + p)),
                  pl.BlockSpec((1, tc, LANES), lambda bb, p, i: (bb, 0, npair + p)),
                  pl.BlockSpec((1, tc, LANES), lambda bb, p, i: (bb, 0, 2 * npair + p)),
                  pl.BlockSpec(lam_params.shape, lambda bb, p, i: (0, 0)),
                  pl.BlockSpec((1, LANES), lambda bb, p, i: (0, 0))],
        out_specs=[pl.BlockSpec((1, tq, LANES), lambda bb, p, i: (bb, jnp.minimum(i, nstep - 1), p)),
                   pl.BlockSpec((1, tq, LANES), lambda bb, p, i: (bb, jnp.maximum(i - 1, 0), p))],
        out_shape=[half, half],
        scratch_shapes=[pltpu.VMEM((t + tc, LANES), BF16), pltpu.VMEM((t + tc, 2 * LANES), BF16),
                        s_scr, s_scr, m_scr, m_scr],
        compiler_params=_cparams(("arbitrary", "arbitrary", "arbitrary")),
        name="attention",
    )(z, z, z, zc, zc, lam_params.astype(F32), subln.reshape(1, LANES).astype(F32))


def _moe_kernel(te_ref, tv_ref, pdst_ref, src_ref, nsrc_ref, wg_ref, wu_ref, wd_ref, h_hbm,
                y_hbm, gbuf, hbuf, acc, obuf, gsem, ssem, *, tm):
    del te_ref
    j = pl.program_id(0)
    f = pl.program_id(1)
    nf = pl.num_programs(1)
    slot = j % 2
    other = 1 - slot
    rows_per_step = tm // MOE_HIDDEN_STEPS
    valid = tv_ref[j] > 0
    drain = jnp.logical_and(jnp.logical_not(valid), tv_ref[jnp.maximum(j - 1, 0)] * (j > 0).astype(I32) > 0)

    def gather_row(idx_ref, to_slot, g, k):
        tok = idx_ref[0, 0, g * SUBLANES + k]
        pltpu.make_async_copy(h_hbm.at[tok], gbuf.at[to_slot, :, g, pl.ds(k, 1), :], gsem.at[to_slot]).start()

    def scatter_row(idx_ref, from_slot, g, k):
        row = idx_ref[0, 0, g * SUBLANES + k]
        pltpu.make_async_copy(obuf.at[from_slot, :, g, pl.ds(k, 1), :], y_hbm.at[:, pl.ds(row, 1), :],
                              ssem.at[from_slot]).start()

    def loop_rows(fn):
        def body(g, c):
            for k in range(SUBLANES):
                fn(g, k)
            return c
        lax.fori_loop(0, tm // SUBLANES, body, 0)

    def wait_gather(s):
        pltpu.make_async_copy(gbuf.at[1 - s], gbuf.at[s], gsem.at[s]).wait()

    def wait_scatter(s):
        pltpu.make_async_copy(obuf.at[1 - s], obuf.at[s], ssem.at[s]).wait()

    @pl.when(jnp.logical_and(j == 0, f == 0))
    def _():
        obuf[1] = jnp.zeros(obuf.shape[1:], U32)
        loop_rows(lambda g, k: gather_row(src_ref, 0, g, k))

    @pl.when(jnp.logical_and(jnp.logical_or(valid, drain), f == 0))
    def _():
        wait_gather(slot)

    @pl.when(jnp.logical_and(valid, f == 0))
    def _():
        n_slab = gbuf.shape[1]
        for sl in range(n_slab):
            lo, hi = _unpack_rows(gbuf[slot, sl].reshape(tm, LANES))
            hbuf[:, sl * LANES:(sl + 1) * LANES] = lo.astype(BF16)
            hbuf[:, (n_slab + sl) * LANES:(n_slab + sl + 1) * LANES] = hi.astype(BF16)

    @pl.when(valid)
    def _():
        base = f * (rows_per_step // SUBLANES)
        for k in range(rows_per_step):
            gather_row(nsrc_ref, other, base + k // SUBLANES, k % SUBLANES)
            scatter_row(pdst_ref, other, base + k // SUBLANES, k % SUBLANES)
        h = hbuf[...]
        g = jnp.dot(h, wg_ref[0], preferred_element_type=F32)
        u = jnp.dot(h, wu_ref[0], preferred_element_type=F32)
        a = (_silu(g) * u).astype(BF16)
        part = jnp.dot(a, wd_ref[0], preferred_element_type=F32)

        @pl.when(f == 0)
        def _():
            acc[...] = part

        @pl.when(f > 0)
        def _():
            acc[...] += part

    @pl.when(jnp.logical_and(valid, f == nf - 1))
    def _():
        @pl.when(j > 0)
        def _():
            wait_scatter(slot)

        packed = _pack_rows(acc[...])
        for sl in range(obuf.shape[1]):
            obuf[slot, sl] = packed[:, sl * LANES:(sl + 1) * LANES].reshape(tm // SUBLANES, SUBLANES, LANES)

    @pl.when(jnp.logical_and(drain, f == 0))
    def _():
        loop_rows(lambda g, k: scatter_row(pdst_ref, other, g, k))
        wait_scatter(other)
        wait_scatter(slot)


def _moe(hp, te, tv, src, dst, w_gu, w_down, tm):
    nt = te.shape[0]
    n_tok, n_slab = hp.shape[:2]
    e, d, two_ff = w_gu.shape
    dff = two_ff // 2
    nf = MOE_HIDDEN_STEPS
    tf = dff // nf

    def wf(j, f, te_r, tv_r):
        return jnp.where(tv_r[j] > 0, f, nf - 1)

    def inv_spec(off):
        return pl.BlockSpec((1, 1, tm), lambda j, f, a, c: (jnp.minimum(j + off, nt), 0, 0),
                            memory_space=pltpu.SMEM)

    grid_spec = pltpu.PrefetchScalarGridSpec(
        num_scalar_prefetch=2,
        grid=(nt, nf),
        in_specs=[inv_spec(0), inv_spec(1), inv_spec(2),
                  pl.BlockSpec((1, d, tf), lambda j, f, a, c: (a[j], 0, wf(j, f, a, c))),
                  pl.BlockSpec((1, d, tf), lambda j, f, a, c: (a[j], 0, wf(j, f, a, c) + nf)),
                  pl.BlockSpec((1, tf, d), lambda j, f, a, c: (a[j], wf(j, f, a, c), 0)),
                  pl.BlockSpec(memory_space=pl.ANY)],
        out_specs=pl.BlockSpec(memory_space=pl.ANY),
        scratch_shapes=[pltpu.VMEM((2, n_slab, tm // SUBLANES, SUBLANES, LANES), U32),
                        pltpu.VMEM((tm, d), BF16),
                        pltpu.VMEM((tm, d), F32),
                        pltpu.VMEM((2, n_slab, tm // SUBLANES, SUBLANES, LANES), U32),
                        pltpu.SemaphoreType.DMA((2,)),
                        pltpu.SemaphoreType.DMA((2,))],
    )
    return pl.pallas_call(
        functools.partial(_moe_kernel, tm=tm),
        grid_spec=grid_spec,
        out_shape=jax.ShapeDtypeStruct((n_slab, 2 * n_tok + tm, LANES), U32),
        compiler_params=_cparams(("arbitrary", "arbitrary")),
        name="moe",
    )(te, tv, dst, src, src, w_gu, w_gu, w_down, hp)


def _route_plan(rt, tm, nt):
    t = rt.shape[0]
    e_flat = rt[:, 0:2].astype(I32).reshape(-1)
    bits = (2 * t - 1).bit_length()
    order = jnp.sort((e_flat << bits) | jnp.arange(2 * t, dtype=I32)) & ((1 << bits) - 1)
    counts = jnp.sum((e_flat[:, None] == jnp.arange(N_EXPERTS, dtype=I32)[None, :]).astype(I32), axis=0)
    tiles_e = (counts + tm - 1) // tm
    tile_end = jnp.cumsum(tiles_e)
    n_used = tile_end[-1]
    tid = jnp.arange(nt, dtype=I32)
    tv = (tid < n_used).astype(I32)
    te = jnp.sum((jnp.minimum(tid, n_used - 1)[:, None] >= tile_end[None, :]).astype(I32), axis=1)
    te = jnp.minimum(te, N_EXPERTS - 1)
    first_tile = (tile_end - tiles_e)[te]
    offset = ((tid - first_tile) * tm)[:, None] + jnp.arange(tm, dtype=I32)[None, :]
    live = jnp.logical_and(tv[:, None] > 0, offset < counts[te][:, None])
    entry = jnp.clip((jnp.cumsum(counts) - counts)[te][:, None] + offset, 0, 2 * t - 1)
    inv = jnp.where(live, order[entry.reshape(-1)].reshape(nt, tm), 2 * t).reshape(-1)
    inv = jnp.concatenate([jnp.full((tm,), 2 * t, I32), inv])
    pad_row = 2 * t + jnp.arange(inv.shape[0], dtype=I32) % tm
    src = jnp.minimum(inv >> 1, t - 1)
    dst = jnp.where(inv < 2 * t, (inv & 1) * t + (inv >> 1), pad_row)
    return te, tv, src.reshape(nt + 1, 1, tm), dst.reshape(nt + 1, 1, tm)


def _combine_kernel(y0_ref, y1_ref, rt_ref, x_ref, gpost_ref, mod_ref, o_ref):
    w0 = rt_ref[0, :, 2:3]
    w1 = rt_ref[0, :, 3:4]
    los, his = [], []
    for sl in range(y0_ref.shape[0]):
        lo0, hi0 = _unpack_rows(y0_ref[sl])
        lo1, hi1 = _unpack_rows(y1_ref[sl])
        los.append(w0 * lo0 + w1 * lo1)
        his.append(w0 * hi0 + w1 * hi1)
    y = jnp.concatenate(los + his, axis=1)
    o_ref[0] = x_ref[0] + mod_ref[0, G2:G2 + 1, :] * _rms_rows(y, gpost_ref[...])


def _combine(y2, rt, x, gpost, mod, tm):
    b, t, d = x.shape
    per_b = t // tm
    nblk = b * per_b
    return pl.pallas_call(
        _combine_kernel,
        grid=(b, per_b),
        in_specs=[pl.BlockSpec((d // 2 // LANES, tm, LANES), lambda bb, i: (0, bb * per_b + i, 0)),
                  pl.BlockSpec((d // 2 // LANES, tm, LANES), lambda bb, i: (0, nblk + bb * per_b + i, 0)),
                  pl.BlockSpec((1, tm, LANES), lambda bb, i: (bb, i, 0)),
                  pl.BlockSpec((1, tm, d), lambda bb, i: (bb, i, 0)),
                  pl.BlockSpec((1, d), lambda bb, i: (0, 0)),
                  pl.BlockSpec((1, 6, d), lambda bb, i: (bb, 0, 0))],
        out_specs=pl.BlockSpec((1, tm, d), lambda bb, i: (bb, i, 0)),
        out_shape=jax.ShapeDtypeStruct((b, t, d), F32),
        compiler_params=_cparams(("arbitrary", "arbitrary")),
        name="combine",
    )(y2, y2, rt, x, gpost.reshape(1, d), mod)


def _rope_angles(rows, head_dim):
    pos_r = jnp.repeat(jnp.arange(rows, dtype=F32), GRID_W)
    pos_c = jnp.tile(jnp.arange(GRID_W, dtype=F32), rows)
    axis_dim = head_dim // 2
    freqs = ROPE_BASE ** (-jnp.arange(0, axis_dim, 2, dtype=F32) / axis_dim)
    return pos_r[:, None] * freqs[None, :], pos_c[:, None] * freqs[None, :]


def _rope_tables(rows, head_dim, reps):
    ar, ac = _rope_angles(rows, head_dim)
    cos = jnp.concatenate([jnp.cos(ar), jnp.cos(ar), jnp.cos(ac), jnp.cos(ac)], axis=1)
    zr = jnp.zeros_like(ar)
    s_up = jnp.concatenate([-jnp.sin(ar), zr, -jnp.sin(ac), zr], axis=1)
    s_dn = jnp.concatenate([zr, jnp.sin(ar), zr, jnp.sin(ac)], axis=1)
    return jnp.stack([jnp.tile(cos, (1, reps)), jnp.tile(s_up, (1, reps)), jnp.tile(s_dn, (1, reps))])


def _identity_tables(n_tab, tm, period):
    return jnp.concatenate([jnp.ones((1, tm, period), F32), jnp.zeros((n_tab - 1, tm, period), F32)])


def _tile(t, want):
    return min(t, want)


def kernel(x, c, ctx, c_ctx, l0_w_mod, l0_b_mod, l0_norm_pre_mix, l0_norm_post_mix, l0_norm_pre_ffn, l0_norm_post_ffn, l0_ret_w_in, l0_ret_decay_logit, l0_ret_w_out, l0_ffn_w_gate_up, l0_ffn_w_down, l1_w_mod, l1_b_mod, l1_norm_pre_mix, l1_norm_post_mix, l1_norm_pre_ffn, l1_norm_post_ffn, l1_attn_w_in, l1_attn_lambda, l1_attn_subln, l1_attn_w_out, l1_router_w, l1_router_b, l1_moe_w_gate_up, l1_moe_w_down):
    b, t, d = x.shape
    tc = ctx.shape[1]
    rows = t // GRID_W
    assert t % tc == 0

    n_rows = -(-(b + 1) // 8) * 8
    cc = jnp.concatenate([c, c_ctx[None, :], jnp.zeros((n_rows - b - 1, d), F32)], axis=0)
    lat_row = lambda bb: bb
    ctx_row = lambda bb: b

    tm = 2 * OUTPROJ_ROW_GROUP
    tn = 1024
    assert t % tm == 0

    mod0 = _modulation(cc, l0_w_mod, l0_b_mod)
    w_in0 = l0_ret_w_in.astype(BF16)
    dk = w_in0.shape[1] // (8 * RET_HEADS)
    tab_ret = _rope_tables(rows, dk, 1)
    tab_ret = jnp.stack([tab_ret[0], tab_ret[1] + tab_ret[2]])
    ret_scales = (1.0,) * (RET_HEADS * dk // tn) + (dk ** -0.5,) * (RET_HEADS * dk // tn)
    z0 = _inproj(x, l0_norm_pre_mix, mod0, lat_row, w_in0, tab_ret, True, tm, tn, ret_scales, (dk // 4,))
    zc0 = _inproj(ctx, l0_norm_pre_mix, mod0, ctx_row, w_in0, _identity_tables(2, tc, dk), False, tc, tn,
                  ret_scales, (dk // 4,))
    yf, yb, yfc, ybc = _retention(z0, zc0, l0_ret_decay_logit)

    w_out0 = l0_ret_w_out.astype(BF16)
    x1, h1 = _outproj([yf, yb], w_out0, x, l0_norm_post_mix, l0_norm_pre_ffn, mod0, lat_row, tm)
    c1, hc1 = _outproj([yfc, ybc], w_out0, ctx, l0_norm_post_mix, l0_norm_pre_ffn, mod0, ctx_row, tc)

    w_gu0 = l0_ffn_w_gate_up.astype(BF16)
    w_dn0 = l0_ffn_w_down.astype(BF16)
    x2 = _ffn(h1, w_gu0, w_dn0, x1, l0_norm_post_ffn, mod0, lat_row, tm, FFN_HIDDEN_TILE)
    c2 = _ffn(hc1, w_gu0, w_dn0, c1, l0_norm_post_ffn, mod0, ctx_row, tc, FFN_HIDDEN_TILE)

    lambda_init = 0.8 - 0.6 * math.exp(-0.3 * 1)
    mod1 = _modulation(cc, l1_w_mod, l1_b_mod)
    w_in1 = l1_attn_w_in.astype(BF16)
    tab_diff = _rope_tables(rows, DIFF_HEAD_DIM, LANES // DIFF_HEAD_DIM)
    q4 = DIFF_HEAD_DIM // 4
    diff_scales = (DIFF_HEAD_DIM ** -0.5 * LOG2_E,) * (d // tn) + (1.0,) * (d // tn)
    z1 = _inproj(x2, l1_norm_pre_mix, mod1, lat_row, w_in1, tab_diff, True, tm, tn, diff_scales,
                 (LANES - q4, q4))
    zc1 = _inproj(c2, l1_norm_pre_mix, mod1, ctx_row, w_in1, _identity_tables(3, tc, LANES), False, tc, tn,
                  (1.0,) * (2 * d // tn), (LANES - q4, q4))
    o_even, o_odd = _attention(z1, zc1, l1_attn_lambda, l1_attn_subln, OUTPROJ_ROW_GROUP, lambda_init)

    rw = jnp.zeros((d, LANES), F32).at[:, :N_EXPERTS].set(l1_router_w)
    rb = jnp.full((1, LANES), NEG_BIG, F32).at[0, :N_EXPERTS].set(l1_router_b)
    x3, hp, rt = _outproj([o_even, o_odd], l1_attn_w_out.astype(BF16), x2, l1_norm_post_mix, l1_norm_pre_ffn,
                          mod1, lat_row, tm, router=(rw, rb), y_per_group=True)

    n_tok = b * t
    tm_e = _tile(n_tok, 1024)
    nt = 2 * n_tok // tm_e + N_EXPERTS
    te, tv, src, dst = _route_plan(rt.reshape(n_tok, LANES), tm_e, nt)
    y2 = _moe(hp.reshape(n_tok, d // 2 // LANES, 1, LANES), te, tv, src, dst, l1_moe_w_gate_up.astype(BF16),
              l1_moe_w_down.astype(BF16), tm_e)
    return _combine(y2, rt, x3, l1_norm_post_ffn, mod1, tm)
```

```python
import functools
import math

import jax
import jax.numpy as jnp
from jax import lax
from jax.experimental import pallas as pl
from jax.experimental.pallas import tpu as pltpu

F32 = jnp.float32
BF16 = jnp.bfloat16
U32 = jnp.uint32
I32 = jnp.int32

EPS = 1e-6
GRID_W = 64
ROPE_BASE = 10000.0
RET_HEADS = 4
DIFF_HEADS = 8
DIFF_HEAD_DIM = 64
N_EXPERTS = 8
LANES = 128
SUBLANES = 8
NEG_BIG = -1e30
VMEM_LIMIT = 56 * 1024 * 1024
DMA_UNROLL = 8
MOE_HIDDEN_STEPS = 4
FFN_HIDDEN_TILE = 256
RET_CHUNKS_PER_STEP = 4
OUTPROJ_ROW_GROUP = 256
LOG2_E = 1.4426950408889634

SH1, SC1, G1, SH2, SC2, G2 = range(6)


def _cparams(sem):
    return pltpu.CompilerParams(dimension_semantics=sem, vmem_limit_bytes=VMEM_LIMIT)


def _rms_rows(x, gain):
    return x * lax.rsqrt(jnp.mean(x * x, axis=-1, keepdims=True) + EPS) * gain


def _silu(x):
    return x * jax.nn.sigmoid(x)


def _mod_kernel(c_ref, w_ref, b_ref, o_ref):
    a = _silu(c_ref[...])
    o_ref[...] = jnp.dot(a, w_ref[...], preferred_element_type=F32,
                         precision=lax.Precision.HIGHEST) + b_ref[...]


def _modulation(cc, w_mod, b_mod):
    r, d = cc.shape
    n = w_mod.shape[1]
    tn = n // 6
    out = pl.pallas_call(
        _mod_kernel,
        grid=(n // tn,),
        in_specs=[pl.BlockSpec((r, d), lambda j: (0, 0)),
                  pl.BlockSpec((d, tn), lambda j: (0, j)),
                  pl.BlockSpec((1, tn), lambda j: (0, j))],
        out_specs=pl.BlockSpec((r, tn), lambda j: (0, j)),
        out_shape=jax.ShapeDtypeStruct((r, n), F32),
        compiler_params=_cparams(("arbitrary",)),
        name="mod",
    )(cc, w_mod, b_mod.reshape(1, n))
    return out.reshape(r, 6, d)


def _inproj_kernel(x_ref, g_ref, mod_ref, w_ref, tab_ref, o_ref, *, tn, rope_scales, shifts, period):
    y = _rms_rows(x_ref[0], g_ref[...])
    h = (y * (1.0 + mod_ref[0, SC1:SC1 + 1, :]) + mod_ref[0, SH1:SH1 + 1, :]).astype(BF16)
    for j in range(w_ref.shape[1] // tn):
        acc = jnp.dot(h, w_ref[:, j * tn:(j + 1) * tn], preferred_element_type=F32)
        if j >= len(rope_scales):
            o_ref[0, :, j * tn:(j + 1) * tn] = acc.astype(o_ref.dtype)
            continue
        for k in range(tn // LANES):
            xs = acc[:, k * LANES:(k + 1) * LANES]
            p = (k * LANES) % period
            r = xs * tab_ref[0, :, p:p + LANES]
            for si, sft in enumerate(shifts):
                r = r + pltpu.roll(xs, sft, 1) * tab_ref[1 + si, :, p:p + LANES]
            if rope_scales[j] != 1.0:
                r = r * rope_scales[j]
            c0 = j * tn + k * LANES
            o_ref[0, :, c0:c0 + LANES] = r.astype(o_ref.dtype)


def _inproj(x, gain, mod, mod_row, w, tab, tab_per_tile, tm, tn, rope_scales, shifts):
    b, t, d = x.shape
    n = w.shape[1]
    period = tab.shape[2]
    kern = functools.partial(_inproj_kernel, tn=tn, rope_scales=rope_scales, shifts=shifts, period=period)
    tab_map = (lambda bb, i: (0, i, 0)) if tab_per_tile else (lambda bb, i: (0, 0, 0))
    return pl.pallas_call(
        kern,
        grid=(b, t // tm),
        in_specs=[pl.BlockSpec((1, tm, d), lambda bb, i: (bb, i, 0)),
                  pl.BlockSpec((1, d), lambda bb, i: (0, 0)),
                  pl.BlockSpec((1, 6, d), lambda bb, i: (mod_row(bb), 0, 0)),
                  pl.BlockSpec((d, n), lambda bb, i: (0, 0), pipeline_mode=pl.Buffered(1)),
                  pl.BlockSpec((tab.shape[0], tm, period), tab_map)],
        out_specs=pl.BlockSpec((1, tm, n), lambda bb, i: (bb, i, 0)),
        out_shape=jax.ShapeDtypeStruct((b, t, n), BF16),
        compiler_params=_cparams(("arbitrary", "arbitrary")),
        name="inproj",
    )(x, gain.reshape(1, d), mod, w, tab)


def _ret_kernel(qf, kf, vf, gf, qb, kb, vb, gb, qc, kc, vc, gfc, gbc, intra, qdec, kdec, cdec,
                yf, yb, yfc, ybc, st):
    s = pl.program_id(2)
    chunk = qc.shape[1]

    def direction(d, q_ref, k_ref, v_ref, g_ref, y_ref, sub):
        rs = slice(sub * chunk, (sub + 1) * chunk)
        q = q_ref[0, rs, :]
        k = k_ref[0, rs, :]
        v = v_ref[0, rs, :]
        scores = lax.dot_general(q, k, (((1,), (1,)), ((), ())), preferred_element_type=F32) * intra[d, 0]
        state = st[d]
        o = (jnp.dot(scores.astype(BF16), v, preferred_element_type=F32)
             + qdec[d, 0] * jnp.dot(q, state.astype(BF16), preferred_element_type=F32))
        ks = (k.astype(F32) * kdec[d, 0]).astype(BF16)
        st[d] = state * cdec[d, 0] + lax.dot_general(ks, v, (((0,), (0,)), ((), ())),
                                                     preferred_element_type=F32)
        hn = o * lax.rsqrt(jnp.mean(o * o, axis=-1, keepdims=True) + EPS)
        y_ref[0, rs, :] = (_silu(g_ref[0, rs, :].astype(F32)) * hn).astype(y_ref.dtype)

    @pl.when(s == 0)
    def _():
        st[...] = jnp.zeros(st.shape, F32)
        direction(0, qc, kc, vc, gfc, yfc, 0)
        direction(1, qc, kc, vc, gbc, ybc, 0)

    @pl.when(s > 0)
    def _():
        for c in range(RET_CHUNKS_PER_STEP):
            direction(0, qf, kf, vf, gf, yf, c)
            direction(1, qb, kb, vb, gb, yb, RET_CHUNKS_PER_STEP - 1 - c)


def _retention(z, zc, decay_logit):
    b, t, _ = z.shape
    chunk = zc.shape[1]
    n_lat = t // chunk
    h = RET_HEADS
    dk = z.shape[2] // (8 * h)
    dv = 2 * dk

    log_g = jax.nn.log_sigmoid(decay_logit.astype(F32))
    idx = jnp.arange(chunk, dtype=F32)
    diff = idx[:, None] - idx[None, :]
    lg = log_g[:, :, None, None]
    intra_f = jnp.where(diff >= 0, jnp.exp(lg[0] * jnp.maximum(diff, 0.0)), 0.0)
    intra_b = jnp.where(diff <= 0, jnp.exp(lg[1] * jnp.maximum(-diff, 0.0)), 0.0)
    intra = jnp.stack([intra_f, intra_b])
    qdec = jnp.stack([jnp.exp(log_g[0][:, None] * (idx + 1.0)),
                      jnp.exp(log_g[1][:, None] * (chunk - idx))])[..., None]
    kdec = jnp.stack([jnp.exp(log_g[0][:, None] * (chunk - 1.0 - idx)),
                      jnp.exp(log_g[1][:, None] * idx)])[..., None]
    cdec = jnp.exp(log_g * chunk)[..., None, None]

    cps = RET_CHUNKS_PER_STEP
    assert n_lat % cps == 0
    n_blk = n_lat // cps

    def fwd(s):
        return jnp.maximum(s - 1, 0)

    def bwd(s):
        return n_blk - jnp.maximum(s, 1)

    def spec(width, col0, chunk_of):
        return pl.BlockSpec((1, cps * chunk, width), lambda bb, hh, s: (bb, chunk_of(s), col0 + hh))

    def cspec(width, col0):
        return pl.BlockSpec((1, chunk, width), lambda bb, hh, s: (bb, 0, col0 + hh))

    in_specs = [spec(dk, 0, fwd), spec(dk, h, fwd), spec(dv, h, fwd), spec(dv, 2 * h, fwd),
                spec(dk, 0, bwd), spec(dk, h, bwd), spec(dv, h, bwd), spec(dv, 3 * h, bwd),
                cspec(dk, 0), cspec(dk, h), cspec(dv, h), cspec(dv, 2 * h), cspec(dv, 3 * h),
                pl.BlockSpec((2, 1, chunk, chunk), lambda bb, hh, s: (0, hh, 0, 0)),
                pl.BlockSpec((2, 1, chunk, 1), lambda bb, hh, s: (0, hh, 0, 0)),
                pl.BlockSpec((2, 1, chunk, 1), lambda bb, hh, s: (0, hh, 0, 0)),
                pl.BlockSpec((2, 1, 1, 1), lambda bb, hh, s: (0, hh, 0, 0))]
    out_specs = [pl.BlockSpec((1, cps * chunk, dv), lambda bb, hh, s: (bb, fwd(s), hh)),
                 pl.BlockSpec((1, cps * chunk, dv), lambda bb, hh, s: (bb, bwd(s), hh)),
                 pl.BlockSpec((1, chunk, dv), lambda bb, hh, s: (bb, 0, hh)),
                 pl.BlockSpec((1, chunk, dv), lambda bb, hh, s: (bb, 0, hh))]
    yl = jax.ShapeDtypeStruct((b, t, h * dv), BF16)
    yc = jax.ShapeDtypeStruct((b, chunk, h * dv), BF16)
    return pl.pallas_call(
        _ret_kernel,
        grid=(b, h, n_blk + 1),
        in_specs=in_specs,
        out_specs=out_specs,
        out_shape=[yl, yl, yc, yc],
        scratch_shapes=[pltpu.VMEM((2, dk, dv), F32)],
        compiler_params=_cparams(("arbitrary", "arbitrary", "arbitrary")),
        name="retention",
    )(z, z, z, z, z, z, z, z, zc, zc, zc, zc, zc, intra, qdec, kdec, cdec)


def _pack_rows(h):
    n = h.shape[1] // 2
    bits = lax.bitcast_convert_type(h.astype(BF16).astype(F32), U32)
    return (bits[:, :n] >> 16) | (bits[:, n:] & jnp.uint32(0xFFFF0000))


def _unpack_rows(w):
    lo = lax.bitcast_convert_type(w << 16, F32)
    hi = lax.bitcast_convert_type(w & jnp.uint32(0xFFFF0000), F32)
    return lo, hi


def _outproj_kernel(*refs, n_y, route, y_per_group):
    y_refs = refs[:n_y]
    w_ref, x_ref, gpost_ref, gpre_ref, mod_ref = refs[n_y:n_y + 5]
    rest = refs[n_y + 5:]
    if route:
        rw_ref, rb_ref, xo_ref, hp_ref, rt_ref = rest
    else:
        xo_ref, h_ref = rest
    tm = x_ref.shape[1]
    rows = min(tm, OUTPROJ_ROW_GROUP)
    for c in range(tm // rows):
        rs = slice(c * rows, (c + 1) * rows)
        if y_per_group:
            y = y_refs[c][0]
        else:
            y = y_refs[0][0, rs, :]
            if n_y == 2:
                y = (y.astype(F32) + y_refs[1][0, rs, :].astype(F32)).astype(BF16)
        o = jnp.dot(y, w_ref[...], preferred_element_type=F32)
        xn = x_ref[0, rs, :] + mod_ref[0, G1:G1 + 1, :] * _rms_rows(o, gpost_ref[...])
        xo_ref[0, rs, :] = xn
        h = _rms_rows(xn, gpre_ref[...]) * (1.0 + mod_ref[0, SC2:SC2 + 1, :]) + mod_ref[0, SH2:SH2 + 1, :]
        if not route:
            h_ref[0, rs, :] = h.astype(h_ref.dtype)
            continue
        packed = _pack_rows(h)
        for sl in range(packed.shape[1] // LANES):
            hp_ref[0, rs, sl, 0, :] = packed[:, sl * LANES:(sl + 1) * LANES]
        logits = jnp.dot(h, rw_ref[...], preferred_element_type=F32,
                         precision=lax.Precision.HIGHEST) + rb_ref[...]
        lane = lax.broadcasted_iota(I32, logits.shape, 1).astype(F32)
        m1 = jnp.max(logits, axis=-1, keepdims=True)
        i1 = jnp.min(jnp.where(logits == m1, lane, float(LANES)), axis=-1, keepdims=True)
        masked = jnp.where(lane == i1, NEG_BIG, logits)
        m2 = jnp.max(masked, axis=-1, keepdims=True)
        i2 = jnp.min(jnp.where(masked == m2, lane, float(LANES)), axis=-1, keepdims=True)
        e2 = jnp.exp(m2 - m1)
        den = 1.0 + e2
        w1 = 1.0 / den
        w2 = e2 / den
        rt_ref[0, rs, :] = jnp.where(lane == 0.0, i1,
                                     jnp.where(lane == 1.0, i2,
                                               jnp.where(lane == 2.0, w1, jnp.where(lane == 3.0, w2, 0.0))))


def _outproj(ys, w, x, gpost, gpre, mod, mod_row, tm, router=None, y_per_group=False):
    b, t, d = x.shape
    kdim = w.shape[0]
    route = router is not None
    ytm = min(tm, OUTPROJ_ROW_GROUP) if y_per_group else tm
    in_specs = [pl.BlockSpec((1, ytm, kdim), lambda bb, i: (bb, i, 0)) for _ in ys]
    in_specs += [pl.BlockSpec((kdim, d), lambda bb, i: (0, 0)),
                 pl.BlockSpec((1, tm, d), lambda bb, i: (bb, i, 0)),
                 pl.BlockSpec((1, d), lambda bb, i: (0, 0)),
                 pl.BlockSpec((1, d), lambda bb, i: (0, 0)),
                 pl.BlockSpec((1, 6, d), lambda bb, i: (mod_row(bb), 0, 0))]
    args = list(ys) + [w, x, gpost.reshape(1, d), gpre.reshape(1, d), mod]
    out_specs = [pl.BlockSpec((1, tm, d), lambda bb, i: (bb, i, 0))]
    out_shape = [jax.ShapeDtypeStruct((b, t, d), F32)]
    if route:
        rw, rb = router
        in_specs += [pl.BlockSpec((d, LANES), lambda bb, i: (0, 0)),
                     pl.BlockSpec((1, LANES), lambda bb, i: (0, 0))]
        args += [rw, rb]
        out_specs += [pl.BlockSpec((1, tm, d // 2 // LANES, 1, LANES), lambda bb, i: (bb, i, 0, 0, 0)),
                      pl.BlockSpec((1, tm, LANES), lambda bb, i: (bb, i, 0))]
        out_shape += [jax.ShapeDtypeStruct((b, t, d // 2 // LANES, 1, LANES), U32),
                      jax.ShapeDtypeStruct((b, t, LANES), F32)]
    else:
        out_specs.append(pl.BlockSpec((1, tm, d), lambda bb, i: (bb, i, 0)))
        out_shape.append(jax.ShapeDtypeStruct((b, t, d), BF16))
    return pl.pallas_call(
        functools.partial(_outproj_kernel, n_y=len(ys), route=route, y_per_group=y_per_group),
        grid=(b, t // tm),
        in_specs=in_specs,
        out_specs=out_specs,
        out_shape=out_shape,
        compiler_params=_cparams(("arbitrary", "arbitrary")),
        name="outproj",
    )(*args)


def _ffn_kernel(h_ref, wgu_ref, wd_ref, x_ref, gpost_ref, mod_ref, o_ref, *, tf):
    h = h_ref[0]
    dff = wd_ref.shape[0]
    acc = None
    for c in range(dff // tf):
        g = jnp.dot(h, wgu_ref[:, c * tf:(c + 1) * tf], preferred_element_type=F32)
        u = jnp.dot(h, wgu_ref[:, dff + c * tf:dff + (c + 1) * tf], preferred_element_type=F32)
        a = (_silu(g) * u).astype(BF16)
        part = jnp.dot(a, wd_ref[c * tf:(c + 1) * tf, :], preferred_element_type=F32)
        acc = part if acc is None else acc + part
    o_ref[0] = x_ref[0] + mod_ref[0, G2:G2 + 1, :] * _rms_rows(acc, gpost_ref[...])


def _ffn(h, w_gu, w_down, x, gpost, mod, mod_row, tm, tf):
    b, t, d = x.shape
    dff = w_down.shape[0]
    return pl.pallas_call(
        functools.partial(_ffn_kernel, tf=tf),
        grid=(b, t // tm),
        in_specs=[pl.BlockSpec((1, tm, d), lambda bb, i: (bb, i, 0)),
                  pl.BlockSpec((d, 2 * dff), lambda bb, i: (0, 0), pipeline_mode=pl.Buffered(1)),
                  pl.BlockSpec((dff, d), lambda bb, i: (0, 0), pipeline_mode=pl.Buffered(1)),
                  pl.BlockSpec((1, tm, d), lambda bb, i: (bb, i, 0)),
                  pl.BlockSpec((1, d), lambda bb, i: (0, 0)),
                  pl.BlockSpec((1, 6, d), lambda bb, i: (mod_row(bb), 0, 0))],
        out_specs=pl.BlockSpec((1, tm, d), lambda bb, i: (bb, i, 0)),
        out_shape=jax.ShapeDtypeStruct((b, t, d), F32),
        compiler_params=_cparams(("arbitrary", "arbitrary")),
        name="ffn",
    )(h, w_gu, w_down, x, gpost.reshape(1, d), mod)


def _attn_kernel(q_ref, kl_ref, vl_ref, kc_ref, vc_ref, lam_ref, subln_ref, oe_ref, oo_ref, ks, vs, sa, sb, ma, mb,
                 *, lambda_init, nstep, tq):
    i = pl.program_id(2)
    t = kl_ref.shape[1]

    @pl.when(i == 0)
    def _():
        ks[0:t, :] = kl_ref[0]
        ks[t:, :] = kc_ref[0]
        vs[0:t, 0:LANES] = vl_ref[0]
        vs[t:, 0:LANES] = vc_ref[0]
        vs[:, LANES:] = jnp.ones((vs.shape[0], LANES), BF16)

    def scores(half, s_scr, m_scr):
        q = q_ref[0, half * tq:(half + 1) * tq, :]
        k = ks[...]
        lane = lax.broadcasted_iota(I32, q.shape, 1)
        zero = jnp.zeros_like(q)
        for h, qh in enumerate((jnp.where(lane < DIFF_HEAD_DIM, q, zero),
                                jnp.where(lane >= DIFF_HEAD_DIM, q, zero))):
            s = lax.dot_general(qh, k, (((1,), (1,)), ((), ())), preferred_element_type=F32)
            s_scr[h] = s
            m_scr[h] = jnp.max(s, axis=-1, keepdims=True)

    def finish(s_scr, m_scr, o_ref):
        v = vs[...]
        lp = lam_ref[...]
        lam = (jnp.exp(jnp.sum(lp[0:1] * lp[1:2], axis=-1, keepdims=True))
               - jnp.exp(jnp.sum(lp[2:3] * lp[3:4], axis=-1, keepdims=True)) + lambda_init)
        outs = []
        for h in range(2):
            p = jnp.exp2(s_scr[h] - m_scr[h]).astype(BF16)
            r = jnp.dot(p, v, preferred_element_type=F32)
            outs.append(r[:, 0:LANES] / r[:, LANES:LANES + 1])
        o = outs[0] - lam * outs[1]
        o_ref[0] = (_rms_rows(o, subln_ref[...]) * (1.0 - lambda_init)).astype(o_ref.dtype)

    @pl.when(i == 0)
    def _():
        scores(0, sa, ma)

    @pl.when(jnp.logical_and(i > 0, i < nstep))
    def _():
        scores(0, sa, ma)
        finish(sb, mb, oo_ref)

    @pl.when(i < nstep)
    def _():
        scores(1, sb, mb)
        finish(sa, ma, oe_ref)

    @pl.when(i == nstep)
    def _():
        finish(sb, mb, oo_ref)


def _attention(z, zc, lam_params, subln, tq, lambda_init):
    b, t, n3 = z.shape
    tc = zc.shape[1]
    d = n3 // 3
    npair = d // LANES
    nstep = t // (2 * tq)
    s_scr = pltpu.VMEM((2, tq, t + tc), F32)
    m_scr = pltpu.VMEM((2, tq, 1), F32)
    half = jax.ShapeDtypeStruct((b, t // 2, d), BF16)
    return pl.pallas_call(
        functools.partial(_attn_kernel, lambda_init=lambda_init, nstep=nstep, tq=tq),
        grid=(b, npair, nstep + 1),
        in_specs=[pl.BlockSpec((1, 2 * tq, LANES), lambda bb, p, i: (bb, jnp.minimum(i, nstep - 1), p)),
                  pl.BlockSpec((1, t, LANES), lambda bb, p, i: (bb, 0, npair + p)),
                  pl.BlockSpec((1, t, LANES), lambda bb, p, i: (bb, 0, 2 * npair + p)),
                  pl.BlockSpec((1, tc, LANES), lambda bb, p, i: (bb, 0, npair + p)),
                  pl.BlockSpec((1, tc, LANES), lambda bb, p, i: (bb, 0, 2 * npair + p)),
                  pl.BlockSpec(lam_params.shape, lambda bb, p, i: (0, 0)),
                  pl.BlockSpec((1, LANES), lambda bb, p, i: (0, 0))],
        out_specs=[pl.BlockSpec((1, tq, LANES), lambda bb, p, i: (bb, jnp.minimum(i, nstep - 1), p)),
                   pl.BlockSpec((1, tq, LANES), lambda bb, p, i: (bb, jnp.maximum(i - 1, 0), p))],
        out_shape=[half, half],
        scratch_shapes=[pltpu.VMEM((t + tc, LANES), BF16), pltpu.VMEM((t + tc, 2 * LANES), BF16),
                        s_scr, s_scr, m_scr, m_scr],
        compiler_params=_cparams(("arbitrary", "arbitrary", "arbitrary")),
        name="attention",
    )(z, z, z, zc, zc, lam_params.astype(F32), subln.reshape(1, LANES).astype(F32))


def _moe_kernel(te_ref, tv_ref, pdst_ref, src_ref, nsrc_ref, wg_ref, wu_ref, wd_ref, h_hbm,
                y_hbm, gbuf, hbuf, acc, obuf, gsem, ssem, *, tm):
    del te_ref
    j = pl.program_id(0)
    f = pl.program_id(1)
    nf = pl.num_programs(1)
    slot = j % 2
    other = 1 - slot
    rows_per_step = tm // MOE_HIDDEN_STEPS
    valid = tv_ref[j] > 0
    drain = jnp.logical_and(jnp.logical_not(valid), tv_ref[jnp.maximum(j - 1, 0)] * (j > 0).astype(I32) > 0)

    def gather_row(idx_ref, to_slot, g, k):
        tok = idx_ref[0, 0, g * SUBLANES + k]
        pltpu.make_async_copy(h_hbm.at[tok], gbuf.at[to_slot, :, g, pl.ds(k, 1), :], gsem.at[to_slot]).start()

    def scatter_row(idx_ref, from_slot, g, k):
        row = idx_ref[0, 0, g * SUBLANES + k]
        pltpu.make_async_copy(obuf.at[from_slot, :, g, pl.ds(k, 1), :], y_hbm.at[:, pl.ds(row, 1), :],
                              ssem.at[from_slot]).start()

    def loop_rows(fn):
        def body(g, c):
            for k in range(SUBLANES):
                fn(g, k)
            return c
        lax.fori_loop(0, tm // SUBLANES, body, 0)

    def wait_gather(s):
        pltpu.make_async_copy(gbuf.at[1 - s], gbuf.at[s], gsem.at[s]).wait()

    def wait_scatter(s):
        pltpu.make_async_copy(obuf.at[1 - s], obuf.at[s], ssem.at[s]).wait()

    @pl.when(jnp.logical_and(j == 0, f == 0))
    def _():
        obuf[1] = jnp.zeros(obuf.shape[1:], U32)
        loop_rows(lambda g, k: gather_row(src_ref, 0, g, k))

    @pl.when(jnp.logical_and(jnp.logical_or(valid, drain), f == 0))
    def _():
        wait_gather(slot)

    @pl.when(jnp.logical_and(valid, f == 0))
    def _():
        n_slab = gbuf.shape[1]
        for sl in range(n_slab):
            lo, hi = _unpack_rows(gbuf[slot, sl].reshape(tm, LANES))
            hbuf[:, sl * LANES:(sl + 1) * LANES] = lo.astype(BF16)
            hbuf[:, (n_slab + sl) * LANES:(n_slab + sl + 1) * LANES] = hi.astype(BF16)
        acc[...] = jnp.zeros(acc.shape, F32)

    @pl.when(valid)
    def _():
        base = f * (rows_per_step // SUBLANES)
        for k in range(rows_per_step):
            gather_row(nsrc_ref, other, base + k // SUBLANES, k % SUBLANES)
            scatter_row(pdst_ref, other, base + k // SUBLANES, k % SUBLANES)
        h = hbuf[...]
        g = jnp.dot(h, wg_ref[0], preferred_element_type=F32)
        u = jnp.dot(h, wu_ref[0], preferred_element_type=F32)
        a = (_silu(g) * u).astype(BF16)
        acc[...] += jnp.dot(a, wd_ref[0], preferred_element_type=F32)

    @pl.when(jnp.logical_and(valid, f == nf - 1))
    def _():
        @pl.when(j > 0)
        def _():
            wait_scatter(slot)

        packed = _pack_rows(acc[...])
        for sl in range(obuf.shape[1]):
            obuf[slot, sl] = packed[:, sl * LANES:(sl + 1) * LANES].reshape(tm // SUBLANES, SUBLANES, LANES)

    @pl.when(jnp.logical_and(drain, f == 0))
    def _():
        loop_rows(lambda g, k: scatter_row(pdst_ref, other, g, k))
        wait_scatter(other)
        wait_scatter(slot)


def _moe(hp, te, tv, src, dst, w_gu, w_down, tm):
    nt = te.shape[0]
    n_tok, n_slab = hp.shape[:2]
    e, d, two_ff = w_gu.shape
    dff = two_ff // 2
    nf = MOE_HIDDEN_STEPS
    tf = dff // nf

    def wf(j, f, te_r, tv_r):
        return jnp.where(tv_r[j] > 0, f, nf - 1)

    def inv_spec(off):
        return pl.BlockSpec((1, 1, tm), lambda j, f, a, c: (jnp.minimum(j + off, nt), 0, 0),
                            memory_space=pltpu.SMEM)

    grid_spec = pltpu.PrefetchScalarGridSpec(
        num_scalar_prefetch=2,
        grid=(nt, nf),
        in_specs=[inv_spec(0), inv_spec(1), inv_spec(2),
                  pl.BlockSpec((1, d, tf), lambda j, f, a, c: (a[j], 0, wf(j, f, a, c))),
                  pl.BlockSpec((1, d, tf), lambda j, f, a, c: (a[j], 0, wf(j, f, a, c) + nf)),
                  pl.BlockSpec((1, tf, d), lambda j, f, a, c: (a[j], wf(j, f, a, c), 0)),
                  pl.BlockSpec(memory_space=pl.ANY)],
        out_specs=pl.BlockSpec(memory_space=pl.ANY),
        scratch_shapes=[pltpu.VMEM((2, n_slab, tm // SUBLANES, SUBLANES, LANES), U32),
                        pltpu.VMEM((tm, d), BF16),
                        pltpu.VMEM((tm, d), F32),
                        pltpu.VMEM((2, n_slab, tm // SUBLANES, SUBLANES, LANES), U32),
                        pltpu.SemaphoreType.DMA((2,)),
                        pltpu.SemaphoreType.DMA((2,))],
    )
    return pl.pallas_call(
        functools.partial(_moe_kernel, tm=tm),
        grid_spec=grid_spec,
        out_shape=jax.ShapeDtypeStruct((n_slab, 2 * n_tok + tm, LANES), U32),
        compiler_params=_cparams(("arbitrary", "arbitrary")),
        name="moe",
    )(te, tv, dst, src, src, w_gu, w_gu, w_down, hp)


def _route_plan(rt, tm, nt):
    t = rt.shape[0]
    e_flat = rt[:, 0:2].astype(I32).reshape(-1)
    bits = (2 * t - 1).bit_length()
    order = jnp.sort((e_flat << bits) | jnp.arange(2 * t, dtype=I32)) & ((1 << bits) - 1)
    counts = jnp.sum((e_flat[:, None] == jnp.arange(N_EXPERTS, dtype=I32)[None, :]).astype(I32), axis=0)
    tiles_e = (counts + tm - 1) // tm
    tile_end = jnp.cumsum(tiles_e)
    n_used = tile_end[-1]
    tid = jnp.arange(nt, dtype=I32)
    tv = (tid < n_used).astype(I32)
    te = jnp.sum((jnp.minimum(tid, n_used - 1)[:, None] >= tile_end[None, :]).astype(I32), axis=1)
    te = jnp.minimum(te, N_EXPERTS - 1)
    first_tile = (tile_end - tiles_e)[te]
    offset = ((tid - first_tile) * tm)[:, None] + jnp.arange(tm, dtype=I32)[None, :]
    live = jnp.logical_and(tv[:, None] > 0, offset < counts[te][:, None])
    entry = jnp.clip((jnp.cumsum(counts) - counts)[te][:, None] + offset, 0, 2 * t - 1)
    inv = jnp.where(live, order[entry.reshape(-1)].reshape(nt, tm), 2 * t).reshape(-1)
    inv = jnp.concatenate([jnp.full((tm,), 2 * t, I32), inv])
    pad_row = 2 * t + jnp.arange(inv.shape[0], dtype=I32) % tm
    src = jnp.minimum(inv >> 1, t - 1)
    dst = jnp.where(inv < 2 * t, (inv & 1) * t + (inv >> 1), pad_row)
    return te, tv, src.reshape(nt + 1, 1, tm), dst.reshape(nt + 1, 1, tm)


def _combine_kernel(y0_ref, y1_ref, rt_ref, x_ref, gpost_ref, mod_ref, o_ref):
    w0 = rt_ref[0, :, 2:3]
    w1 = rt_ref[0, :, 3:4]
    los, his = [], []
    for sl in range(y0_ref.shape[0]):
        lo0, hi0 = _unpack_rows(y0_ref[sl])
        lo1, hi1 = _unpack_rows(y1_ref[sl])
        los.append(w0 * lo0 + w1 * lo1)
        his.append(w0 * hi0 + w1 * hi1)
    y = jnp.concatenate(los + his, axis=1)
    o_ref[0] = x_ref[0] + mod_ref[0, G2:G2 + 1, :] * _rms_rows(y, gpost_ref[...])


def _combine(y2, rt, x, gpost, mod, tm):
    b, t, d = x.shape
    per_b = t // tm
    nblk = b * per_b
    return pl.pallas_call(
        _combine_kernel,
        grid=(b, per_b),
        in_specs=[pl.BlockSpec((d // 2 // LANES, tm, LANES), lambda bb, i: (0, bb * per_b + i, 0)),
                  pl.BlockSpec((d // 2 // LANES, tm, LANES), lambda bb, i: (0, nblk + bb * per_b + i, 0)),
                  pl.BlockSpec((1, tm, LANES), lambda bb, i: (bb, i, 0)),
                  pl.BlockSpec((1, tm, d), lambda bb, i: (bb, i, 0)),
                  pl.BlockSpec((1, d), lambda bb, i: (0, 0)),
                  pl.BlockSpec((1, 6, d), lambda bb, i: (bb, 0, 0))],
        out_specs=pl.BlockSpec((1, tm, d), lambda bb, i: (bb, i, 0)),
        out_shape=jax.ShapeDtypeStruct((b, t, d), F32),
        compiler_params=_cparams(("arbitrary", "arbitrary")),
        name="combine",
    )(y2, y2, rt, x, gpost.reshape(1, d), mod)


def _rope_angles(rows, head_dim):
    pos_r = jnp.repeat(jnp.arange(rows, dtype=F32), GRID_W)
    pos_c = jnp.tile(jnp.arange(GRID_W, dtype=F32), rows)
    axis_dim = head_dim // 2
    freqs = ROPE_BASE ** (-jnp.arange(0, axis_dim, 2, dtype=F32) / axis_dim)
    return pos_r[:, None] * freqs[None, :], pos_c[:, None] * freqs[None, :]


def _rope_tables(rows, head_dim, reps):
    ar, ac = _rope_angles(rows, head_dim)
    cos = jnp.concatenate([jnp.cos(ar), jnp.cos(ar), jnp.cos(ac), jnp.cos(ac)], axis=1)
    zr = jnp.zeros_like(ar)
    s_up = jnp.concatenate([-jnp.sin(ar), zr, -jnp.sin(ac), zr], axis=1)
    s_dn = jnp.concatenate([zr, jnp.sin(ar), zr, jnp.sin(ac)], axis=1)
    return jnp.stack([jnp.tile(cos, (1, reps)), jnp.tile(s_up, (1, reps)), jnp.tile(s_dn, (1, reps))])


def _identity_tables(n_tab, tm, period):
    return jnp.concatenate([jnp.ones((1, tm, period), F32), jnp.zeros((n_tab - 1, tm, period), F32)])


def _tile(t, want):
    return min(t, want)


def kernel(x, c, ctx, c_ctx, l0_w_mod, l0_b_mod, l0_norm_pre_mix, l0_norm_post_mix, l0_norm_pre_ffn, l0_norm_post_ffn, l0_ret_w_in, l0_ret_decay_logit, l0_ret_w_out, l0_ffn_w_gate_up, l0_ffn_w_down, l1_w_mod, l1_b_mod, l1_norm_pre_mix, l1_norm_post_mix, l1_norm_pre_ffn, l1_norm_post_ffn, l1_attn_w_in, l1_attn_lambda, l1_attn_subln, l1_attn_w_out, l1_router_w, l1_router_b, l1_moe_w_gate_up, l1_moe_w_down):
    b, t, d = x.shape
    tc = ctx.shape[1]
    rows = t // GRID_W
    assert t % tc == 0

    n_rows = -(-(b + 1) // 8) * 8
    cc = jnp.concatenate([c, c_ctx[None, :], jnp.zeros((n_rows - b - 1, d), F32)], axis=0)
    lat_row = lambda bb: bb
    ctx_row = lambda bb: b

    tm = 2 * OUTPROJ_ROW_GROUP
    tn = 1024
    assert t % tm == 0

    mod0 = _modulation(cc, l0_w_mod, l0_b_mod)
    w_in0 = l0_ret_w_in.astype(BF16)
    dk = w_in0.shape[1] // (8 * RET_HEADS)
    tab_ret = _rope_tables(rows, dk, 1)
    tab_ret = jnp.stack([tab_ret[0], tab_ret[1] + tab_ret[2]])
    ret_scales = (1.0,) * (RET_HEADS * dk // tn) + (dk ** -0.5,) * (RET_HEADS * dk // tn)
    z0 = _inproj(x, l0_norm_pre_mix, mod0, lat_row, w_in0, tab_ret, True, tm, tn, ret_scales, (dk // 4,))
    zc0 = _inproj(ctx, l0_norm_pre_mix, mod0, ctx_row, w_in0, _identity_tables(2, tc, dk), False, tc, tn,
                  ret_scales, (dk // 4,))
    yf, yb, yfc, ybc = _retention(z0, zc0, l0_ret_decay_logit)

    w_out0 = l0_ret_w_out.astype(BF16)
    x1, h1 = _outproj([yf, yb], w_out0, x, l0_norm_post_mix, l0_norm_pre_ffn, mod0, lat_row, tm)
    c1, hc1 = _outproj([yfc, ybc], w_out0, ctx, l0_norm_post_mix, l0_norm_pre_ffn, mod0, ctx_row, tc)

    w_gu0 = l0_ffn_w_gate_up.astype(BF16)
    w_dn0 = l0_ffn_w_down.astype(BF16)
    x2 = _ffn(h1, w_gu0, w_dn0, x1, l0_norm_post_ffn, mod0, lat_row, tm, FFN_HIDDEN_TILE)
    c2 = _ffn(hc1, w_gu0, w_dn0, c1, l0_norm_post_ffn, mod0, ctx_row, tc, FFN_HIDDEN_TILE)

    lambda_init = 0.8 - 0.6 * math.exp(-0.3 * 1)
    mod1 = _modulation(cc, l1_w_mod, l1_b_mod)
    w_in1 = l1_attn_w_in.astype(BF16)
    tab_diff = _rope_tables(rows, DIFF_HEAD_DIM, LANES // DIFF_HEAD_DIM)
    q4 = DIFF_HEAD_DIM // 4
    diff_scales = (DIFF_HEAD_DIM ** -0.5 * LOG2_E,) * (d // tn) + (1.0,) * (d // tn)
    z1 = _inproj(x2, l1_norm_pre_mix, mod1, lat_row, w_in1, tab_diff, True, tm, tn, diff_scales,
                 (LANES - q4, q4))
    zc1 = _inproj(c2, l1_norm_pre_mix, mod1, ctx_row, w_in1, _identity_tables(3, tc, LANES), False, tc, tn,
                  (1.0,) * (2 * d // tn), (LANES - q4, q4))
    o_even, o_odd = _attention(z1, zc1, l1_attn_lambda, l1_attn_subln, OUTPROJ_ROW_GROUP, lambda_init)

    rw = jnp.zeros((d, LANES), F32).at[:, :N_EXPERTS].set(l1_router_w)
    rb = jnp.full((1, LANES), NEG_BIG, F32).at[0, :N_EXPERTS].set(l1_router_b)
    x3, hp, rt = _outproj([o_even, o_odd], l1_attn_w_out.astype(BF16), x2, l1_norm_post_mix, l1_norm_pre_ffn,
                          mod1, lat_row, tm, router=(rw, rb), y_per_group=True)

    n_tok = b * t
    tm_e = _tile(n_tok, 1024)
    nt = 2 * n_tok // tm_e + N_EXPERTS
    te, tv, src, dst = _route_plan(rt.reshape(n_tok, LANES), tm_e, nt)
    y2 = _moe(hp.reshape(n_tok, d // 2 // LANES, 1, LANES), te, tv, src, dst, l1_moe_w_gate_up.astype(BF16),
              l1_moe_w_down.astype(BF16), tm_e)
    return _combine(y2, rt, x3, l1_norm_post_ffn, mod1, tm)
```

```python
import functools
import math

import jax
import jax.numpy as jnp
from jax import lax
from jax.experimental import pallas as pl
from jax.experimental.pallas import tpu as pltpu

F32 = jnp.float32
BF16 = jnp.bfloat16
U32 = jnp.uint32
I32 = jnp.int32

EPS = 1e-6
GRID_W = 64
ROPE_BASE = 10000.0
RET_HEADS = 4
DIFF_HEADS = 8
DIFF_HEAD_DIM = 64
N_EXPERTS = 8
LANES = 128
SUBLANES = 8
NEG_BIG = -1e30
VMEM_LIMIT = 56 * 1024 * 1024
DMA_UNROLL = 8
MOE_HIDDEN_STEPS = 4
FFN_HIDDEN_TILE = 256
RET_CHUNKS_PER_STEP = 4
OUTPROJ_ROW_GROUP = 256
LOG2_E = 1.4426950408889634

SH1, SC1, G1, SH2, SC2, G2 = range(6)


def _cparams(sem):
    return pltpu.CompilerParams(dimension_semantics=sem, vmem_limit_bytes=VMEM_LIMIT)


def _rms_rows(x, gain):
    return x * lax.rsqrt(jnp.mean(x * x, axis=-1, keepdims=True) + EPS) * gain


def _silu(x):
    return x * jax.nn.sigmoid(x)


def _mod_kernel(c_ref, w_ref, b_ref, o_ref):
    a = _silu(c_ref[...])
    o_ref[...] = jnp.dot(a, w_ref[...], preferred_element_type=F32,
                         precision=lax.Precision.HIGHEST) + b_ref[...]


def _modulation(cc, w_mod, b_mod):
    r, d = cc.shape
    n = w_mod.shape[1]
    tn = n // 6
    out = pl.pallas_call(
        _mod_kernel,
        grid=(n // tn,),
        in_specs=[pl.BlockSpec((r, d), lambda j: (0, 0)),
                  pl.BlockSpec((d, tn), lambda j: (0, j)),
                  pl.BlockSpec((1, tn), lambda j: (0, j))],
        out_specs=pl.BlockSpec((r, tn), lambda j: (0, j)),
        out_shape=jax.ShapeDtypeStruct((r, n), F32),
        compiler_params=_cparams(("arbitrary",)),
        name="mod",
    )(cc, w_mod, b_mod.reshape(1, n))
    return out.reshape(r, 6, d)


def _inproj_kernel(x_ref, g_ref, mod_ref, w_ref, tab_ref, o_ref, *, tn, rope_scales, shifts, period):
    y = _rms_rows(x_ref[0], g_ref[...])
    h = (y * (1.0 + mod_ref[0, SC1:SC1 + 1, :]) + mod_ref[0, SH1:SH1 + 1, :]).astype(BF16)
    for j in range(w_ref.shape[1] // tn):
        acc = jnp.dot(h, w_ref[:, j * tn:(j + 1) * tn], preferred_element_type=F32)
        if j >= len(rope_scales):
            o_ref[0, :, j * tn:(j + 1) * tn] = acc.astype(o_ref.dtype)
            continue
        for k in range(tn // LANES):
            xs = acc[:, k * LANES:(k + 1) * LANES]
            p = (k * LANES) % period
            r = xs * tab_ref[0, :, p:p + LANES]
            for si, sft in enumerate(shifts):
                r = r + pltpu.roll(xs, sft, 1) * tab_ref[1 + si, :, p:p + LANES]
            if rope_scales[j] != 1.0:
                r = r * rope_scales[j]
            c0 = j * tn + k * LANES
            o_ref[0, :, c0:c0 + LANES] = r.astype(o_ref.dtype)


def _inproj(x, gain, mod, mod_row, w, tab, tab_per_tile, tm, tn, rope_scales, shifts):
    b, t, d = x.shape
    n = w.shape[1]
    period = tab.shape[2]
    kern = functools.partial(_inproj_kernel, tn=tn, rope_scales=rope_scales, shifts=shifts, period=period)
    tab_map = (lambda bb, i: (0, i, 0)) if tab_per_tile else (lambda bb, i: (0, 0, 0))
    return pl.pallas_call(
        kern,
        grid=(b, t // tm),
        in_specs=[pl.BlockSpec((1, tm, d), lambda bb, i: (bb, i, 0)),
                  pl.BlockSpec((1, d), lambda bb, i: (0, 0)),
                  pl.BlockSpec((1, 6, d), lambda bb, i: (mod_row(bb), 0, 0)),
                  pl.BlockSpec((d, n), lambda bb, i: (0, 0), pipeline_mode=pl.Buffered(1)),
                  pl.BlockSpec((tab.shape[0], tm, period), tab_map)],
        out_specs=pl.BlockSpec((1, tm, n), lambda bb, i: (bb, i, 0)),
        out_shape=jax.ShapeDtypeStruct((b, t, n), BF16),
        compiler_params=_cparams(("arbitrary", "arbitrary")),
        name="inproj",
    )(x, gain.reshape(1, d), mod, w, tab)


def _ret_kernel(qf, kf, vf, gf, qb, kb, vb, gb, qc, kc, vc, gfc, gbc, intra, qdec, kdec, cdec,
                yf, yb, yfc, ybc, st):
    s = pl.program_id(2)
    chunk = qc.shape[1]

    def direction(d, q_ref, k_ref, v_ref, g_ref, y_ref, sub):
        rs = slice(sub * chunk, (sub + 1) * chunk)
        q = q_ref[0, rs, :]
        k = k_ref[0, rs, :]
        v = v_ref[0, rs, :]
        scores = lax.dot_general(q, k, (((1,), (1,)), ((), ())), preferred_element_type=F32) * intra[d, 0]
        state = st[d]
        o = (jnp.dot(scores.astype(BF16), v, preferred_element_type=F32)
             + qdec[d, 0] * jnp.dot(q, state.astype(BF16), preferred_element_type=F32))
        ks = (k.astype(F32) * kdec[d, 0]).astype(BF16)
        st[d] = state * cdec[d, 0] + lax.dot_general(ks, v, (((0,), (0,)), ((), ())),
                                                     preferred_element_type=F32)
        hn = o * lax.rsqrt(jnp.mean(o * o, axis=-1, keepdims=True) + EPS)
        y_ref[0, rs, :] = (_silu(g_ref[0, rs, :].astype(F32)) * hn).astype(y_ref.dtype)

    @pl.when(s == 0)
    def _():
        st[...] = jnp.zeros(st.shape, F32)
        direction(0, qc, kc, vc, gfc, yfc, 0)
        direction(1, qc, kc, vc, gbc, ybc, 0)

    @pl.when(s > 0)
    def _():
        for c in range(RET_CHUNKS_PER_STEP):
            direction(0, qf, kf, vf, gf, yf, c)
            direction(1, qb, kb, vb, gb, yb, RET_CHUNKS_PER_STEP - 1 - c)


def _retention(z, zc, decay_logit):
    b, t, _ = z.shape
    chunk = zc.shape[1]
    n_lat = t // chunk
    h = RET_HEADS
    dk = z.shape[2] // (8 * h)
    dv = 2 * dk

    log_g = jax.nn.log_sigmoid(decay_logit.astype(F32))
    idx = jnp.arange(chunk, dtype=F32)
    diff = idx[:, None] - idx[None, :]
    lg = log_g[:, :, None, None]
    intra_f = jnp.where(diff >= 0, jnp.exp(lg[0] * jnp.maximum(diff, 0.0)), 0.0)
    intra_b = jnp.where(diff <= 0, jnp.exp(lg[1] * jnp.maximum(-diff, 0.0)), 0.0)
    intra = jnp.stack([intra_f, intra_b])
    qdec = jnp.stack([jnp.exp(log_g[0][:, None] * (idx + 1.0)),
                      jnp.exp(log_g[1][:, None] * (chunk - idx))])[..., None]
    kdec = jnp.stack([jnp.exp(log_g[0][:, None] * (chunk - 1.0 - idx)),
                      jnp.exp(log_g[1][:, None] * idx)])[..., None]
    cdec = jnp.exp(log_g * chunk)[..., None, None]

    cps = RET_CHUNKS_PER_STEP
    assert n_lat % cps == 0
    n_blk = n_lat // cps

    def fwd(s):
        return jnp.maximum(s - 1, 0)

    def bwd(s):
        return n_blk - jnp.maximum(s, 1)

    def spec(width, col0, chunk_of):
        return pl.BlockSpec((1, cps * chunk, width), lambda bb, hh, s: (bb, chunk_of(s), col0 + hh))

    def cspec(width, col0):
        return pl.BlockSpec((1, chunk, width), lambda bb, hh, s: (bb, 0, col0 + hh))

    in_specs = [spec(dk, 0, fwd), spec(dk, h, fwd), spec(dv, h, fwd), spec(dv, 2 * h, fwd),
                spec(dk, 0, bwd), spec(dk, h, bwd), spec(dv, h, bwd), spec(dv, 3 * h, bwd),
                cspec(dk, 0), cspec(dk, h), cspec(dv, h), cspec(dv, 2 * h), cspec(dv, 3 * h),
                pl.BlockSpec((2, 1, chunk, chunk), lambda bb, hh, s: (0, hh, 0, 0)),
                pl.BlockSpec((2, 1, chunk, 1), lambda bb, hh, s: (0, hh, 0, 0)),
                pl.BlockSpec((2, 1, chunk, 1), lambda bb, hh, s: (0, hh, 0, 0)),
                pl.BlockSpec((2, 1, 1, 1), lambda bb, hh, s: (0, hh, 0, 0))]
    out_specs = [pl.BlockSpec((1, cps * chunk, dv), lambda bb, hh, s: (bb, fwd(s), hh)),
                 pl.BlockSpec((1, cps * chunk, dv), lambda bb, hh, s: (bb, bwd(s), hh)),
                 pl.BlockSpec((1, chunk, dv), lambda bb, hh, s: (bb, 0, hh)),
                 pl.BlockSpec((1, chunk, dv), lambda bb, hh, s: (bb, 0, hh))]
    yl = jax.ShapeDtypeStruct((b, t, h * dv), BF16)
    yc = jax.ShapeDtypeStruct((b, chunk, h * dv), BF16)
    return pl.pallas_call(
        _ret_kernel,
        grid=(b, h, n_blk + 1),
        in_specs=in_specs,
        out_specs=out_specs,
        out_shape=[yl, yl, yc, yc],
        scratch_shapes=[pltpu.VMEM((2, dk, dv), F32)],
        compiler_params=_cparams(("arbitrary", "arbitrary", "arbitrary")),
        name="retention",
    )(z, z, z, z, z, z, z, z, zc, zc, zc, zc, zc, intra, qdec, kdec, cdec)


def _pack_rows(h):
    n = h.shape[1] // 2
    bits = lax.bitcast_convert_type(h.astype(BF16).astype(F32), U32)
    return (bits[:, :n] >> 16) | (bits[:, n:] & jnp.uint32(0xFFFF0000))


def _unpack_rows(w):
    lo = lax.bitcast_convert_type(w << 16, F32)
    hi = lax.bitcast_convert_type(w & jnp.uint32(0xFFFF0000), F32)
    return lo, hi


def _outproj_kernel(*refs, n_y, route, y_per_group):
    y_refs = refs[:n_y]
    w_ref, x_ref, gpost_ref, gpre_ref, mod_ref = refs[n_y:n_y + 5]
    rest = refs[n_y + 5:]
    if route:
        rw_ref, rb_ref, xo_ref, hp_ref, rt_ref = rest
    else:
        xo_ref, h_ref = rest
    tm = x_ref.shape[1]
    rows = min(tm, OUTPROJ_ROW_GROUP)
    for c in range(tm // rows):
        rs = slice(c * rows, (c + 1) * rows)
        if y_per_group:
            y = y_refs[c][0]
        else:
            y = y_refs[0][0, rs, :]
            if n_y == 2:
                y = (y.astype(F32) + y_refs[1][0, rs, :].astype(F32)).astype(BF16)
        o = jnp.dot(y, w_ref[...], preferred_element_type=F32)
        xn = x_ref[0, rs, :] + mod_ref[0, G1:G1 + 1, :] * _rms_rows(o, gpost_ref[...])
        xo_ref[0, rs, :] = xn
        h = _rms_rows(xn, gpre_ref[...]) * (1.0 + mod_ref[0, SC2:SC2 + 1, :]) + mod_ref[0, SH2:SH2 + 1, :]
        if not route:
            h_ref[0, rs, :] = h.astype(h_ref.dtype)
            continue
        packed = _pack_rows(h)
        for sl in range(packed.shape[1] // LANES):
            hp_ref[0, rs, sl, 0, :] = packed[:, sl * LANES:(sl + 1) * LANES]
        logits = jnp.dot(h, rw_ref[...], preferred_element_type=F32,
                         precision=lax.Precision.HIGHEST) + rb_ref[...]
        lane = lax.broadcasted_iota(I32, logits.shape, 1).astype(F32)
        m1 = jnp.max(logits, axis=-1, keepdims=True)
        i1 = jnp.min(jnp.where(logits == m1, lane, float(LANES)), axis=-1, keepdims=True)
        masked = jnp.where(lane == i1, NEG_BIG, logits)
        m2 = jnp.max(masked, axis=-1, keepdims=True)
        i2 = jnp.min(jnp.where(masked == m2, lane, float(LANES)), axis=-1, keepdims=True)
        e2 = jnp.exp(m2 - m1)
        den = 1.0 + e2
        w1 = 1.0 / den
        w2 = e2 / den
        rt_ref[0, rs, :] = jnp.where(lane == 0.0, i1,
                                     jnp.where(lane == 1.0, i2,
                                               jnp.where(lane == 2.0, w1, jnp.where(lane == 3.0, w2, 0.0))))


def _outproj(ys, w, x, gpost, gpre, mod, mod_row, tm, router=None, y_per_group=False):
    b, t, d = x.shape
    kdim = w.shape[0]
    route = router is not None
    ytm = min(tm, OUTPROJ_ROW_GROUP) if y_per_group else tm
    in_specs = [pl.BlockSpec((1, ytm, kdim), lambda bb, i: (bb, i, 0)) for _ in ys]
    in_specs += [pl.BlockSpec((kdim, d), lambda bb, i: (0, 0)),
                 pl.BlockSpec((1, tm, d), lambda bb, i: (bb, i, 0)),
                 pl.BlockSpec((1, d), lambda bb, i: (0, 0)),
                 pl.BlockSpec((1, d), lambda bb, i: (0, 0)),
                 pl.BlockSpec((1, 6, d), lambda bb, i: (mod_row(bb), 0, 0))]
    args = list(ys) + [w, x, gpost.reshape(1, d), gpre.reshape(1, d), mod]
    out_specs = [pl.BlockSpec((1, tm, d), lambda bb, i: (bb, i, 0))]
    out_shape = [jax.ShapeDtypeStruct((b, t, d), F32)]
    if route:
        rw, rb = router
        in_specs += [pl.BlockSpec((d, LANES), lambda bb, i: (0, 0)),
                     pl.BlockSpec((1, LANES), lambda bb, i: (0, 0))]
        args += [rw, rb]
        out_specs += [pl.BlockSpec((1, tm, d // 2 // LANES, 1, LANES), lambda bb, i: (bb, i, 0, 0, 0)),
                      pl.BlockSpec((1, tm, LANES), lambda bb, i: (bb, i, 0))]
        out_shape += [jax.ShapeDtypeStruct((b, t, d // 2 // LANES, 1, LANES), U32),
                      jax.ShapeDtypeStruct((b, t, LANES), F32)]
    else:
        out_specs.append(pl.BlockSpec((1, tm, d), lambda bb, i: (bb, i, 0)))
        out_shape.append(jax.ShapeDtypeStruct((b, t, d), BF16))
    return pl.pallas_call(
        functools.partial(_outproj_kernel, n_y=len(ys), route=route, y_per_group=y_per_group),
        grid=(b, t // tm),
        in_specs=in_specs,
        out_specs=out_specs,
        out_shape=out_shape,
        compiler_params=_cparams(("arbitrary", "arbitrary")),
        name="outproj",
    )(*args)


def _ffn_kernel(h_ref, wgu_ref, wd_ref, x_ref, gpost_ref, mod_ref, o_ref, *, tf):
    h = h_ref[0]
    dff = wd_ref.shape[0]
    acc = None
    for c in range(dff // tf):
        g = jnp.dot(h, wgu_ref[:, c * tf:(c + 1) * tf], preferred_element_type=F32)
        u = jnp.dot(h, wgu_ref[:, dff + c * tf:dff + (c + 1) * tf], preferred_element_type=F32)
        a = (_silu(g) * u).astype(BF16)
        part = jnp.dot(a, wd_ref[c * tf:(c + 1) * tf, :], preferred_element_type=F32)
        acc = part if acc is None else acc + part
    o_ref[0] = x_ref[0] + mod_ref[0, G2:G2 + 1, :] * _rms_rows(acc, gpost_ref[...])


def _ffn(h, w_gu, w_down, x, gpost, mod, mod_row, tm, tf):
    b, t, d = x.shape
    dff = w_down.shape[0]
    return pl.pallas_call(
        functools.partial(_ffn_kernel, tf=tf),
        grid=(b, t // tm),
        in_specs=[pl.BlockSpec((1, tm, d), lambda bb, i: (bb, i, 0)),
                  pl.BlockSpec((d, 2 * dff), lambda bb, i: (0, 0), pipeline_mode=pl.Buffered(1)),
                  pl.BlockSpec((dff, d), lambda bb, i: (0, 0), pipeline_mode=pl.Buffered(1)),
                  pl.BlockSpec((1, tm, d), lambda bb, i: (bb, i, 0)),
                  pl.BlockSpec((1, d), lambda bb, i: (0, 0)),
                  pl.BlockSpec((1, 6, d), lambda bb, i: (mod_row(bb), 0, 0))],
        out_specs=pl.BlockSpec((1, tm, d), lambda bb, i: (bb, i, 0)),
        out_shape=jax.ShapeDtypeStruct((b, t, d), F32),
        compiler_params=_cparams(("arbitrary", "arbitrary")),
        name="ffn",
    )(h, w_gu, w_down, x, gpost.reshape(1, d), mod)


def _attn_kernel(q_ref, kl_ref, vl_ref, kc_ref, vc_ref, lam_ref, subln_ref, oe_ref, oo_ref, ks, vs, sa, sb, ma, mb,
                 *, lambda_init, nstep, tq):
    i = pl.program_id(2)
    t = kl_ref.shape[1]

    @pl.when(i == 0)
    def _():
        ks[0:t, :] = kl_ref[0]
        ks[t:, :] = kc_ref[0]
        vs[0:t, 0:LANES] = vl_ref[0]
        vs[t:, 0:LANES] = vc_ref[0]
        vs[:, LANES:] = jnp.ones((vs.shape[0], LANES), BF16)

    def scores(half, s_scr, m_scr):
        q = q_ref[0, half * tq:(half + 1) * tq, :]
        k = ks[...]
        lane = lax.broadcasted_iota(I32, q.shape, 1)
        zero = jnp.zeros_like(q)
        for h, qh in enumerate((jnp.where(lane < DIFF_HEAD_DIM, q, zero),
                                jnp.where(lane >= DIFF_HEAD_DIM, q, zero))):
            s = lax.dot_general(qh, k, (((1,), (1,)), ((), ())), preferred_element_type=F32)
            s_scr[h] = s
            m_scr[h] = jnp.max(s, axis=-1, keepdims=True)

    def finish(s_scr, m_scr, o_ref):
        v = vs[...]
        lp = lam_ref[...]
        lam = (jnp.exp(jnp.sum(lp[0:1] * lp[1:2], axis=-1, keepdims=True))
               - jnp.exp(jnp.sum(lp[2:3] * lp[3:4], axis=-1, keepdims=True)) + lambda_init)
        outs = []
        for h in range(2):
            p = jnp.exp2(s_scr[h] - m_scr[h]).astype(BF16)
            r = jnp.dot(p, v, preferred_element_type=F32)
            outs.append(r[:, 0:LANES] / r[:, LANES:LANES + 1])
        o = outs[0] - lam * outs[1]
        o_ref[0] = (_rms_rows(o, subln_ref[...]) * (1.0 - lambda_init)).astype(o_ref.dtype)

    @pl.when(i == 0)
    def _():
        scores(0, sa, ma)

    @pl.when(jnp.logical_and(i > 0, i < nstep))
    def _():
        scores(0, sa, ma)
        finish(sb, mb, oo_ref)

    @pl.when(i < nstep)
    def _():
        scores(1, sb, mb)
        finish(sa, ma, oe_ref)

    @pl.when(i == nstep)
    def _():
        finish(sb, mb, oo_ref)


def _attention(z, zc, lam_params, subln, tq, lambda_init):
    b, t, n3 = z.shape
    tc = zc.shape[1]
    d = n3 // 3
    npair = d // LANES
    nstep = t // (2 * tq)
    s_scr = pltpu.VMEM((2, tq, t + tc), F32)
    m_scr = pltpu.VMEM((2, tq, 1), F32)
    half = jax.ShapeDtypeStruct((b, t // 2, d), BF16)
    return pl.pallas_call(
        functools.partial(_attn_kernel, lambda_init=lambda_init, nstep=nstep, tq=tq),
        grid=(b, npair, nstep + 1),
        in_specs=[pl.BlockSpec((1, 2 * tq, LANES), lambda bb, p, i: (bb, jnp.minimum(i, nstep - 1), p)),
                  pl.BlockSpec((1, t, LANES), lambda bb, p, i: (bb, 0, npair + p)),
                  pl.BlockSpec((1, t, LANES), lambda bb, p, i: (bb, 0, 2 * npair + p)),
                  pl.BlockSpec((1, tc, LANES), lambda bb, p, i: (bb, 0, npair + p)),
                  pl.BlockSpec((1, tc, LANES), lambda bb, p, i: (bb, 0, 2 * npair + p)),
                  pl.BlockSpec(lam_params.shape, lambda bb, p, i: (0, 0)),
                  pl.BlockSpec((1, LANES), lambda bb, p, i: (0, 0))],
        out_specs=[pl.BlockSpec((1, tq, LANES), lambda bb, p, i: (bb, jnp.minimum(i, nstep - 1), p)),
                   pl.BlockSpec((1, tq, LANES), lambda bb, p, i: (bb, jnp.maximum(i - 1, 0), p))],
        out_shape=[half, half],
        scratch_shapes=[pltpu.VMEM((t + tc, LANES), BF16), pltpu.VMEM((t + tc, 2 * LANES), BF16),
                        s_scr, s_scr, m_scr, m_scr],
        compiler_params=_cparams(("arbitrary", "arbitrary", "arbitrary")),
        name="attention",
    )(z, z, z, zc, zc, lam_params.astype(F32), subln.reshape(1, LANES).astype(F32))


def _moe_kernel(te_ref, tv_ref, pdst_ref, src_ref, nsrc_ref, wg_ref, wu_ref, wd_ref, h_hbm,
                y_hbm, gbuf, hbuf, acc, obuf, gsem, ssem, *, tm):
    del te_ref
    j = pl.program_id(0)
    f = pl.program_id(1)
    nf = pl.num_programs(1)
    slot = j % 2
    other = 1 - slot
    rows_per_step = tm // MOE_HIDDEN_STEPS
    valid = tv_ref[j] > 0
    drain = jnp.logical_and(jnp.logical_not(valid), tv_ref[jnp.maximum(j - 1, 0)] * (j > 0).astype(I32) > 0)

    def gather_row(idx_ref, to_slot, g, k):
        tok = idx_ref[0, 0, g * SUBLANES + k]
        pltpu.make_async_copy(h_hbm.at[tok], gbuf.at[to_slot, :, g, pl.ds(k, 1), :], gsem.at[to_slot]).start()

    def scatter_row(idx_ref, from_slot, g, k):
        row = idx_ref[0, 0, g * SUBLANES + k]
        pltpu.make_async_copy(obuf.at[from_slot, :, g, pl.ds(k, 1), :], y_hbm.at[:, pl.ds(row, 1), :],
                              ssem.at[from_slot]).start()

    def loop_rows(fn):
        def body(g, c):
            for k in range(SUBLANES):
                fn(g, k)
            return c
        lax.fori_loop(0, tm // SUBLANES, body, 0)

    def wait_gather(s):
        pltpu.make_async_copy(gbuf.at[1 - s], gbuf.at[s], gsem.at[s]).wait()

    def wait_scatter(s):
        pltpu.make_async_copy(obuf.at[1 - s], obuf.at[s], ssem.at[s]).wait()

    @pl.when(jnp.logical_and(j == 0, f == 0))
    def _():
        obuf[1] = jnp.zeros(obuf.shape[1:], U32)
        loop_rows(lambda g, k: gather_row(src_ref, 0, g, k))

    @pl.when(jnp.logical_and(jnp.logical_or(valid, drain), f == 0))
    def _():
        wait_gather(slot)

    @pl.when(jnp.logical_and(valid, f == 0))
    def _():
        n_slab = gbuf.shape[1]
        for sl in range(n_slab):
            lo, hi = _unpack_rows(gbuf[slot, sl].reshape(tm, LANES))
            hbuf[:, sl * LANES:(sl + 1) * LANES] = lo.astype(BF16)
            hbuf[:, (n_slab + sl) * LANES:(n_slab + sl + 1) * LANES] = hi.astype(BF16)
        acc[...] = jnp.zeros(acc.shape, F32)

    @pl.when(valid)
    def _():
        base = f * (rows_per_step // SUBLANES)
        for k in range(rows_per_step):
            scatter_row(pdst_ref, other, base + k // SUBLANES, k % SUBLANES)

        @pl.when(2 * f < nf)
        def _():
            gbase = 2 * base
            for k in range(2 * rows_per_step):
                gather_row(nsrc_ref, other, gbase + k // SUBLANES, k % SUBLANES)

        h = hbuf[...]
        g = jnp.dot(h, wg_ref[0], preferred_element_type=F32)
        u = jnp.dot(h, wu_ref[0], preferred_element_type=F32)
        a = (_silu(g) * u).astype(BF16)
        acc[...] += jnp.dot(a, wd_ref[0], preferred_element_type=F32)

    @pl.when(jnp.logical_and(valid, f == nf - 1))
    def _():
        @pl.when(j > 0)
        def _():
            wait_scatter(slot)

        packed = _pack_rows(acc[...])
        for sl in range(obuf.shape[1]):
            obuf[slot, sl] = packed[:, sl * LANES:(sl + 1) * LANES].reshape(tm // SUBLANES, SUBLANES, LANES)

    @pl.when(jnp.logical_and(drain, f == 0))
    def _():
        loop_rows(lambda g, k: scatter_row(pdst_ref, other, g, k))
        wait_scatter(other)
        wait_scatter(slot)


def _moe(hp, te, tv, src, dst, w_gu, w_down, tm):
    nt = te.shape[0]
    n_tok, n_slab = hp.shape[:2]
    e, d, two_ff = w_gu.shape
    dff = two_ff // 2
    nf = MOE_HIDDEN_STEPS
    tf = dff // nf

    def wf(j, f, te_r, tv_r):
        return jnp.where(tv_r[j] > 0, f, nf - 1)

    def inv_spec(off):
        return pl.BlockSpec((1, 1, tm), lambda j, f, a, c: (jnp.minimum(j + off, nt), 0, 0),
                            memory_space=pltpu.SMEM)

    grid_spec = pltpu.PrefetchScalarGridSpec(
        num_scalar_prefetch=2,
        grid=(nt, nf),
        in_specs=[inv_spec(0), inv_spec(1), inv_spec(2),
                  pl.BlockSpec((1, d, tf), lambda j, f, a, c: (a[j], 0, wf(j, f, a, c))),
                  pl.BlockSpec((1, d, tf), lambda j, f, a, c: (a[j], 0, wf(j, f, a, c) + nf)),
                  pl.BlockSpec((1, tf, d), lambda j, f, a, c: (a[j], wf(j, f, a, c), 0)),
                  pl.BlockSpec(memory_space=pl.ANY)],
        out_specs=pl.BlockSpec(memory_space=pl.ANY),
        scratch_shapes=[pltpu.VMEM((2, n_slab, tm // SUBLANES, SUBLANES, LANES), U32),
                        pltpu.VMEM((tm, d), BF16),
                        pltpu.VMEM((tm, d), F32),
                        pltpu.VMEM((2, n_slab, tm // SUBLANES, SUBLANES, LANES), U32),
                        pltpu.SemaphoreType.DMA((2,)),
                        pltpu.SemaphoreType.DMA((2,))],
    )
    return pl.pallas_call(
        functools.partial(_moe_kernel, tm=tm),
        grid_spec=grid_spec,
        out_shape=jax.ShapeDtypeStruct((n_slab, 2 * n_tok + tm, LANES), U32),
        compiler_params=_cparams(("arbitrary", "arbitrary")),
        name="moe",
    )(te, tv, dst, src, src, w_gu, w_gu, w_down, hp)


def _route_plan(rt, tm, nt):
    t = rt.shape[0]
    e_flat = rt[:, 0:2].astype(I32).reshape(-1)
    bits = (2 * t - 1).bit_length()
    order = jnp.sort((e_flat << bits) | jnp.arange(2 * t, dtype=I32)) & ((1 << bits) - 1)
    counts = jnp.sum((e_flat[:, None] == jnp.arange(N_EXPERTS, dtype=I32)[None, :]).astype(I32), axis=0)
    tiles_e = (counts + tm - 1) // tm
    tile_end = jnp.cumsum(tiles_e)
    n_used = tile_end[-1]
    tid = jnp.arange(nt, dtype=I32)
    tv = (tid < n_used).astype(I32)
    te = jnp.sum((jnp.minimum(tid, n_used - 1)[:, None] >= tile_end[None, :]).astype(I32), axis=1)
    te = jnp.minimum(te, N_EXPERTS - 1)
    first_tile = (tile_end - tiles_e)[te]
    offset = ((tid - first_tile) * tm)[:, None] + jnp.arange(tm, dtype=I32)[None, :]
    live = jnp.logical_and(tv[:, None] > 0, offset < counts[te][:, None])
    entry = jnp.clip((jnp.cumsum(counts) - counts)[te][:, None] + offset, 0, 2 * t - 1)
    inv = jnp.where(live, order[entry.reshape(-1)].reshape(nt, tm), 2 * t).reshape(-1)
    inv = jnp.concatenate([jnp.full((tm,), 2 * t, I32), inv])
    pad_row = 2 * t + jnp.arange(inv.shape[0], dtype=I32) % tm
    src = jnp.minimum(inv >> 1, t - 1)
    dst = jnp.where(inv < 2 * t, (inv & 1) * t + (inv >> 1), pad_row)
    return te, tv, src.reshape(nt + 1, 1, tm), dst.reshape(nt + 1, 1, tm)


def _combine_kernel(y0_ref, y1_ref, rt_ref, x_ref, gpost_ref, mod_ref, o_ref):
    w0 = rt_ref[0, :, 2:3]
    w1 = rt_ref[0, :, 3:4]
    los, his = [], []
    for sl in range(y0_ref.shape[0]):
        lo0, hi0 = _unpack_rows(y0_ref[sl])
        lo1, hi1 = _unpack_rows(y1_ref[sl])
        los.append(w0 * lo0 + w1 * lo1)
        his.append(w0 * hi0 + w1 * hi1)
    y = jnp.concatenate(los + his, axis=1)
    o_ref[0] = x_ref[0] + mod_ref[0, G2:G2 + 1, :] * _rms_rows(y, gpost_ref[...])


def _combine(y2, rt, x, gpost, mod, tm):
    b, t, d = x.shape
    per_b = t // tm
    nblk = b * per_b
    return pl.pallas_call(
        _combine_kernel,
        grid=(b, per_b),
        in_specs=[pl.BlockSpec((d // 2 // LANES, tm, LANES), lambda bb, i: (0, bb * per_b + i, 0)),
                  pl.BlockSpec((d // 2 // LANES, tm, LANES), lambda bb, i: (0, nblk + bb * per_b + i, 0)),
                  pl.BlockSpec((1, tm, LANES), lambda bb, i: (bb, i, 0)),
                  pl.BlockSpec((1, tm, d), lambda bb, i: (bb, i, 0)),
                  pl.BlockSpec((1, d), lambda bb, i: (0, 0)),
                  pl.BlockSpec((1, 6, d), lambda bb, i: (bb, 0, 0))],
        out_specs=pl.BlockSpec((1, tm, d), lambda bb, i: (bb, i, 0)),
        out_shape=jax.ShapeDtypeStruct((b, t, d), F32),
        compiler_params=_cparams(("arbitrary", "arbitrary")),
        name="combine",
    )(y2, y2, rt, x, gpost.reshape(1, d), mod)


def _rope_angles(rows, head_dim):
    pos_r = jnp.repeat(jnp.arange(rows, dtype=F32), GRID_W)
    pos_c = jnp.tile(jnp.arange(GRID_W, dtype=F32), rows)
    axis_dim = head_dim // 2
    freqs = ROPE_BASE ** (-jnp.arange(0, axis_dim, 2, dtype=F32) / axis_dim)
    return pos_r[:, None] * freqs[None, :], pos_c[:, None] * freqs[None, :]


def _rope_tables(rows, head_dim, reps):
    ar, ac = _rope_angles(rows, head_dim)
    cos = jnp.concatenate([jnp.cos(ar), jnp.cos(ar), jnp.cos(ac), jnp.cos(ac)], axis=1)
    zr = jnp.zeros_like(ar)
    s_up = jnp.concatenate([-jnp.sin(ar), zr, -jnp.sin(ac), zr], axis=1)
    s_dn = jnp.concatenate([zr, jnp.sin(ar), zr, jnp.sin(ac)], axis=1)
    return jnp.stack([jnp.tile(cos, (1, reps)), jnp.tile(s_up, (1, reps)), jnp.tile(s_dn, (1, reps))])


def _identity_tables(n_tab, tm, period):
    return jnp.concatenate([jnp.ones((1, tm, period), F32), jnp.zeros((n_tab - 1, tm, period), F32)])


def _tile(t, want):
    return min(t, want)


def kernel(x, c, ctx, c_ctx, l0_w_mod, l0_b_mod, l0_norm_pre_mix, l0_norm_post_mix, l0_norm_pre_ffn, l0_norm_post_ffn, l0_ret_w_in, l0_ret_decay_logit, l0_ret_w_out, l0_ffn_w_gate_up, l0_ffn_w_down, l1_w_mod, l1_b_mod, l1_norm_pre_mix, l1_norm_post_mix, l1_norm_pre_ffn, l1_norm_post_ffn, l1_attn_w_in, l1_attn_lambda, l1_attn_subln, l1_attn_w_out, l1_router_w, l1_router_b, l1_moe_w_gate_up, l1_moe_w_down):
    b, t, d = x.shape
    tc = ctx.shape[1]
    rows = t // GRID_W
    assert t % tc == 0

    n_rows = -(-(b + 1) // 8) * 8
    cc = jnp.concatenate([c, c_ctx[None, :], jnp.zeros((n_rows - b - 1, d), F32)], axis=0)
    lat_row = lambda bb: bb
    ctx_row = lambda bb: b

    tm = 2 * OUTPROJ_ROW_GROUP
    tn = 1024
    assert t % tm == 0

    mod0 = _modulation(cc, l0_w_mod, l0_b_mod)
    w_in0 = l0_ret_w_in.astype(BF16)
    dk = w_in0.shape[1] // (8 * RET_HEADS)
    tab_ret = _rope_tables(rows, dk, 1)
    tab_ret = jnp.stack([tab_ret[0], tab_ret[1] + tab_ret[2]])
    ret_scales = (1.0,) * (RET_HEADS * dk // tn) + (dk ** -0.5,) * (RET_HEADS * dk // tn)
    z0 = _inproj(x, l0_norm_pre_mix, mod0, lat_row, w_in0, tab_ret, True, tm, tn, ret_scales, (dk // 4,))
    zc0 = _inproj(ctx, l0_norm_pre_mix, mod0, ctx_row, w_in0, _identity_tables(2, tc, dk), False, tc, tn,
                  ret_scales, (dk // 4,))
    yf, yb, yfc, ybc = _retention(z0, zc0, l0_ret_decay_logit)

    w_out0 = l0_ret_w_out.astype(BF16)
    x1, h1 = _outproj([yf, yb], w_out0, x, l0_norm_post_mix, l0_norm_pre_ffn, mod0, lat_row, tm)
    c1, hc1 = _outproj([yfc, ybc], w_out0, ctx, l0_norm_post_mix, l0_norm_pre_ffn, mod0, ctx_row, tc)

    w_gu0 = l0_ffn_w_gate_up.astype(BF16)
    w_dn0 = l0_ffn_w_down.astype(BF16)
    x2 = _ffn(h1, w_gu0, w_dn0, x1, l0_norm_post_ffn, mod0, lat_row, tm, FFN_HIDDEN_TILE)
    c2 = _ffn(hc1, w_gu0, w_dn0, c1, l0_norm_post_ffn, mod0, ctx_row, tc, FFN_HIDDEN_TILE)

    lambda_init = 0.8 - 0.6 * math.exp(-0.3 * 1)
    mod1 = _modulation(cc, l1_w_mod, l1_b_mod)
    w_in1 = l1_attn_w_in.astype(BF16)
    tab_diff = _rope_tables(rows, DIFF_HEAD_DIM, LANES // DIFF_HEAD_DIM)
    q4 = DIFF_HEAD_DIM // 4
    diff_scales = (DIFF_HEAD_DIM ** -0.5 * LOG2_E,) * (d // tn) + (1.0,) * (d // tn)
    z1 = _inproj(x2, l1_norm_pre_mix, mod1, lat_row, w_in1, tab_diff, True, tm, tn, diff_scales,
                 (LANES - q4, q4))
    zc1 = _inproj(c2, l1_norm_pre_mix, mod1, ctx_row, w_in1, _identity_tables(3, tc, LANES), False, tc, tn,
                  (1.0,) * (2 * d // tn), (LANES - q4, q4))
    o_even, o_odd = _attention(z1, zc1, l1_attn_lambda, l1_attn_subln, OUTPROJ_ROW_GROUP, lambda_init)

    rw = jnp.zeros((d, LANES), F32).at[:, :N_EXPERTS].set(l1_router_w)
    rb = jnp.full((1, LANES), NEG_BIG, F32).at[0, :N_EXPERTS].set(l1_router_b)
    x3, hp, rt = _outproj([o_even, o_odd], l1_attn_w_out.astype(BF16), x2, l1_norm_post_mix, l1_norm_pre_ffn,
                          mod1, lat_row, tm, router=(rw, rb), y_per_group=True)

    n_tok = b * t
    tm_e = _tile(n_tok, 1024)
    nt = 2 * n_tok // tm_e + N_EXPERTS
    te, tv, src, dst = _route_plan(rt.reshape(n_tok, LANES), tm_e, nt)
    y2 = _moe(hp.reshape(n_tok, d // 2 // LANES, 1, LANES), te, tv, src, dst, l1_moe_w_gate_up.astype(BF16),
              l1_moe_w_down.astype(BF16), tm_e)
    return _combine(y2, rt, x3, l1_norm_post_ffn, mod1, tm)
```

```python
import functools
import math

import jax
import jax.numpy as jnp
from jax import lax
from jax.experimental import pallas as pl
from jax.experimental.pallas import tpu as pltpu

F32 = jnp.float32
BF16 = jnp.bfloat16
U32 = jnp.uint32
I32 = jnp.int32

EPS = 1e-6
GRID_W = 64
ROPE_BASE = 10000.0
RET_HEADS = 4
DIFF_HEADS = 8
DIFF_HEAD_DIM = 64
N_EXPERTS = 8
LANES = 128
SUBLANES = 8
NEG_BIG = -1e30
VMEM_LIMIT = 56 * 1024 * 1024
DMA_UNROLL = 8
MOE_HIDDEN_STEPS = 4
FFN_HIDDEN_TILE = 256
RET_CHUNKS_PER_STEP = 4
OUTPROJ_ROW_GROUP = 256
LOG2_E = 1.4426950408889634

SH1, SC1, G1, SH2, SC2, G2 = range(6)


def _cparams(sem):
    return pltpu.CompilerParams(dimension_semantics=sem, vmem_limit_bytes=VMEM_LIMIT)


def _rms_rows(x, gain):
    return x * lax.rsqrt(jnp.mean(x * x, axis=-1, keepdims=True) + EPS) * gain


def _silu(x):
    return x * jax.nn.sigmoid(x)


def _mod_kernel(c_ref, w_ref, b_ref, o_ref):
    a = _silu(c_ref[...])
    o_ref[...] = jnp.dot(a, w_ref[...], preferred_element_type=F32,
                         precision=lax.Precision.HIGHEST) + b_ref[...]


def _modulation(cc, w_mod, b_mod):
    r, d = cc.shape
    n = w_mod.shape[1]
    tn = n // 6
    out = pl.pallas_call(
        _mod_kernel,
        grid=(n // tn,),
        in_specs=[pl.BlockSpec((r, d), lambda j: (0, 0)),
                  pl.BlockSpec((d, tn), lambda j: (0, j)),
                  pl.BlockSpec((1, tn), lambda j: (0, j))],
        out_specs=pl.BlockSpec((r, tn), lambda j: (0, j)),
        out_shape=jax.ShapeDtypeStruct((r, n), F32),
        compiler_params=_cparams(("arbitrary",)),
        name="mod",
    )(cc, w_mod, b_mod.reshape(1, n))
    return out.reshape(r, 6, d)


def _inproj_kernel(x_ref, g_ref, mod_ref, w_ref, tab_ref, o_ref, *, tn, rope_scales, shifts, period):
    y = _rms_rows(x_ref[0], g_ref[...])
    h = (y * (1.0 + mod_ref[0, SC1:SC1 + 1, :]) + mod_ref[0, SH1:SH1 + 1, :]).astype(BF16)
    for j in range(w_ref.shape[1] // tn):
        acc = jnp.dot(h, w_ref[:, j * tn:(j + 1) * tn], preferred_element_type=F32)
        if j >= len(rope_scales):
            o_ref[0, :, j * tn:(j + 1) * tn] = acc.astype(o_ref.dtype)
            continue
        for k in range(tn // LANES):
            xs = acc[:, k * LANES:(k + 1) * LANES]
            p = (k * LANES) % period
            r = xs * tab_ref[0, :, p:p + LANES]
            for si, sft in enumerate(shifts):
                r = r + pltpu.roll(xs, sft, 1) * tab_ref[1 + si, :, p:p + LANES]
            if rope_scales[j] != 1.0:
                r = r * rope_scales[j]
            c0 = j * tn + k * LANES
            o_ref[0, :, c0:c0 + LANES] = r.astype(o_ref.dtype)


def _inproj(x, gain, mod, mod_row, w, tab, tab_per_tile, tm, tn, rope_scales, shifts):
    b, t, d = x.shape
    n = w.shape[1]
    period = tab.shape[2]
    kern = functools.partial(_inproj_kernel, tn=tn, rope_scales=rope_scales, shifts=shifts, period=period)
    tab_map = (lambda bb, i: (0, i, 0)) if tab_per_tile else (lambda bb, i: (0, 0, 0))
    return pl.pallas_call(
        kern,
        grid=(b, t // tm),
        in_specs=[pl.BlockSpec((1, tm, d), lambda bb, i: (bb, i, 0)),
                  pl.BlockSpec((1, d), lambda bb, i: (0, 0)),
                  pl.BlockSpec((1, 6, d), lambda bb, i: (mod_row(bb), 0, 0)),
                  pl.BlockSpec((d, n), lambda bb, i: (0, 0), pipeline_mode=pl.Buffered(1)),
                  pl.BlockSpec((tab.shape[0], tm, period), tab_map)],
        out_specs=pl.BlockSpec((1, tm, n), lambda bb, i: (bb, i, 0)),
        out_shape=jax.ShapeDtypeStruct((b, t, n), BF16),
        compiler_params=_cparams(("arbitrary", "arbitrary")),
        name="inproj",
    )(x, gain.reshape(1, d), mod, w, tab)


def _ret_kernel(qf, kf, vf, gf, qb, kb, vb, gb, qc, kc, vc, gfc, gbc, intra, qdec, kdec, cdec,
                yf, yb, yfc, ybc, st):
    s = pl.program_id(2)
    chunk = qc.shape[1]

    def direction(d, q_ref, k_ref, v_ref, g_ref, y_ref, sub):
        rs = slice(sub * chunk, (sub + 1) * chunk)
        q = q_ref[0, rs, :]
        k = k_ref[0, rs, :]
        v = v_ref[0, rs, :]
        scores = lax.dot_general(q, k, (((1,), (1,)), ((), ())), preferred_element_type=F32) * intra[d, 0]
        state = st[d]
        o = (jnp.dot(scores.astype(BF16), v, preferred_element_type=F32)
             + qdec[d, 0] * jnp.dot(q, state.astype(BF16), preferred_element_type=F32))
        ks = (k.astype(F32) * kdec[d, 0]).astype(BF16)
        st[d] = state * cdec[d, 0] + lax.dot_general(ks, v, (((0,), (0,)), ((), ())),
                                                     preferred_element_type=F32)
        hn = o * lax.rsqrt(jnp.mean(o * o, axis=-1, keepdims=True) + EPS)
        y_ref[0, rs, :] = (_silu(g_ref[0, rs, :].astype(F32)) * hn).astype(y_ref.dtype)

    @pl.when(s == 0)
    def _():
        st[...] = jnp.zeros(st.shape, F32)
        direction(0, qc, kc, vc, gfc, yfc, 0)
        direction(1, qc, kc, vc, gbc, ybc, 0)

    @pl.when(s > 0)
    def _():
        for c in range(RET_CHUNKS_PER_STEP):
            direction(0, qf, kf, vf, gf, yf, c)
            direction(1, qb, kb, vb, gb, yb, RET_CHUNKS_PER_STEP - 1 - c)


def _retention(z, zc, decay_logit):
    b, t, _ = z.shape
    chunk = zc.shape[1]
    n_lat = t // chunk
    h = RET_HEADS
    dk = z.shape[2] // (8 * h)
    dv = 2 * dk

    log_g = jax.nn.log_sigmoid(decay_logit.astype(F32))
    idx = jnp.arange(chunk, dtype=F32)
    diff = idx[:, None] - idx[None, :]
    lg = log_g[:, :, None, None]
    intra_f = jnp.where(diff >= 0, jnp.exp(lg[0] * jnp.maximum(diff, 0.0)), 0.0)
    intra_b = jnp.where(diff <= 0, jnp.exp(lg[1] * jnp.maximum(-diff, 0.0)), 0.0)
    intra = jnp.stack([intra_f, intra_b])
    qdec = jnp.stack([jnp.exp(log_g[0][:, None] * (idx + 1.0)),
                      jnp.exp(log_g[1][:, None] * (chunk - idx))])[..., None]
    kdec = jnp.stack([jnp.exp(log_g[0][:, None] * (chunk - 1.0 - idx)),
                      jnp.exp(log_g[1][:, None] * idx)])[..., None]
    cdec = jnp.exp(log_g * chunk)[..., None, None]

    cps = RET_CHUNKS_PER_STEP
    assert n_lat % cps == 0
    n_blk = n_lat // cps

    def fwd(s):
        return jnp.maximum(s - 1, 0)

    def bwd(s):
        return n_blk - jnp.maximum(s, 1)

    def spec(width, col0, chunk_of):
        return pl.BlockSpec((1, cps * chunk, width), lambda bb, hh, s: (bb, chunk_of(s), col0 + hh))

    def cspec(width, col0):
        return pl.BlockSpec((1, chunk, width), lambda bb, hh, s: (bb, 0, col0 + hh))

    in_specs = [spec(dk, 0, fwd), spec(dk, h, fwd), spec(dv, h, fwd), spec(dv, 2 * h, fwd),
                spec(dk, 0, bwd), spec(dk, h, bwd), spec(dv, h, bwd), spec(dv, 3 * h, bwd),
                cspec(dk, 0), cspec(dk, h), cspec(dv, h), cspec(dv, 2 * h), cspec(dv, 3 * h),
                pl.BlockSpec((2, 1, chunk, chunk), lambda bb, hh, s: (0, hh, 0, 0)),
                pl.BlockSpec((2, 1, chunk, 1), lambda bb, hh, s: (0, hh, 0, 0)),
                pl.BlockSpec((2, 1, chunk, 1), lambda bb, hh, s: (0, hh, 0, 0)),
                pl.BlockSpec((2, 1, 1, 1), lambda bb, hh, s: (0, hh, 0, 0))]
    out_specs = [pl.BlockSpec((1, cps * chunk, dv), lambda bb, hh, s: (bb, fwd(s), hh)),
                 pl.BlockSpec((1, cps * chunk, dv), lambda bb, hh, s: (bb, bwd(s), hh)),
                 pl.BlockSpec((1, chunk, dv), lambda bb, hh, s: (bb, 0, hh)),
                 pl.BlockSpec((1, chunk, dv), lambda bb, hh, s: (bb, 0, hh))]
    yl = jax.ShapeDtypeStruct((b, t, h * dv), BF16)
    yc = jax.ShapeDtypeStruct((b, chunk, h * dv), BF16)
    return pl.pallas_call(
        _ret_kernel,
        grid=(b, h, n_blk + 1),
        in_specs=in_specs,
        out_specs=out_specs,
        out_shape=[yl, yl, yc, yc],
        scratch_shapes=[pltpu.VMEM((2, dk, dv), F32)],
        compiler_params=_cparams(("arbitrary", "arbitrary", "arbitrary")),
        name="retention",
    )(z, z, z, z, z, z, z, z, zc, zc, zc, zc, zc, intra, qdec, kdec, cdec)


def _pack_rows(h):
    n = h.shape[1] // 2
    bits = lax.bitcast_convert_type(h.astype(BF16).astype(F32), U32)
    return (bits[:, :n] >> 16) | (bits[:, n:] & jnp.uint32(0xFFFF0000))


def _unpack_rows(w):
    lo = lax.bitcast_convert_type(w << 16, F32)
    hi = lax.bitcast_convert_type(w & jnp.uint32(0xFFFF0000), F32)
    return lo, hi


def _outproj_kernel(*refs, n_y, route, y_per_group):
    y_refs = refs[:n_y]
    w_ref, x_ref, gpost_ref, gpre_ref, mod_ref = refs[n_y:n_y + 5]
    rest = refs[n_y + 5:]
    if route:
        rw_ref, rb_ref, xo_ref, hp_ref, rt_ref = rest
    else:
        xo_ref, h_ref = rest
    tm = x_ref.shape[1]
    rows = min(tm, OUTPROJ_ROW_GROUP)
    for c in range(tm // rows):
        rs = slice(c * rows, (c + 1) * rows)
        if y_per_group:
            y = y_refs[c][0]
        else:
            y = y_refs[0][0, rs, :]
            if n_y == 2:
                y = (y.astype(F32) + y_refs[1][0, rs, :].astype(F32)).astype(BF16)
        o = jnp.dot(y, w_ref[...], preferred_element_type=F32)
        xn = x_ref[0, rs, :] + mod_ref[0, G1:G1 + 1, :] * _rms_rows(o, gpost_ref[...])
        xo_ref[0, rs, :] = xn
        h = _rms_rows(xn, gpre_ref[...]) * (1.0 + mod_ref[0, SC2:SC2 + 1, :]) + mod_ref[0, SH2:SH2 + 1, :]
        if not route:
            h_ref[0, rs, :] = h.astype(h_ref.dtype)
            continue
        packed = _pack_rows(h)
        for sl in range(packed.shape[1] // LANES):
            hp_ref[0, rs, sl, 0, :] = packed[:, sl * LANES:(sl + 1) * LANES]
        logits = jnp.dot(h, rw_ref[...], preferred_element_type=F32,
                         precision=lax.Precision.HIGHEST) + rb_ref[...]
        lane = lax.broadcasted_iota(I32, logits.shape, 1).astype(F32)
        m1 = jnp.max(logits, axis=-1, keepdims=True)
        i1 = jnp.min(jnp.where(logits == m1, lane, float(LANES)), axis=-1, keepdims=True)
        masked = jnp.where(lane == i1, NEG_BIG, logits)
        m2 = jnp.max(masked, axis=-1, keepdims=True)
        i2 = jnp.min(jnp.where(masked == m2, lane, float(LANES)), axis=-1, keepdims=True)
        e2 = jnp.exp(m2 - m1)
        den = 1.0 + e2
        w1 = 1.0 / den
        w2 = e2 / den
        rt_ref[0, rs, :] = jnp.where(lane == 0.0, i1,
                                     jnp.where(lane == 1.0, i2,
                                               jnp.where(lane == 2.0, w1, jnp.where(lane == 3.0, w2, 0.0))))


def _outproj(ys, w, x, gpost, gpre, mod, mod_row, tm, router=None, y_per_group=False):
    b, t, d = x.shape
    kdim = w.shape[0]
    route = router is not None
    ytm = min(tm, OUTPROJ_ROW_GROUP) if y_per_group else tm
    in_specs = [pl.BlockSpec((1, ytm, kdim), lambda bb, i: (bb, i, 0)) for _ in ys]
    in_specs += [pl.BlockSpec((kdim, d), lambda bb, i: (0, 0)),
                 pl.BlockSpec((1, tm, d), lambda bb, i: (bb, i, 0)),
                 pl.BlockSpec((1, d), lambda bb, i: (0, 0)),
                 pl.BlockSpec((1, d), lambda bb, i: (0, 0)),
                 pl.BlockSpec((1, 6, d), lambda bb, i: (mod_row(bb), 0, 0))]
    args = list(ys) + [w, x, gpost.reshape(1, d), gpre.reshape(1, d), mod]
    out_specs = [pl.BlockSpec((1, tm, d), lambda bb, i: (bb, i, 0))]
    out_shape = [jax.ShapeDtypeStruct((b, t, d), F32)]
    if route:
        rw, rb = router
        in_specs += [pl.BlockSpec((d, LANES), lambda bb, i: (0, 0)),
                     pl.BlockSpec((1, LANES), lambda bb, i: (0, 0))]
        args += [rw, rb]
        out_specs += [pl.BlockSpec((1, tm, d // 2 // LANES, 1, LANES), lambda bb, i: (bb, i, 0, 0, 0)),
                      pl.BlockSpec((1, tm, LANES), lambda bb, i: (bb, i, 0))]
        out_shape += [jax.ShapeDtypeStruct((b, t, d // 2 // LANES, 1, LANES), U32),
                      jax.ShapeDtypeStruct((b, t, LANES), F32)]
    else:
        out_specs.append(pl.BlockSpec((1, tm, d), lambda bb, i: (bb, i, 0)))
        out_shape.append(jax.ShapeDtypeStruct((b, t, d), BF16))
    return pl.pallas_call(
        functools.partial(_outproj_kernel, n_y=len(ys), route=route, y_per_group=y_per_group),
        grid=(b, t // tm),
        in_specs=in_specs,
        out_specs=out_specs,
        out_shape=out_shape,
        compiler_params=_cparams(("arbitrary", "arbitrary")),
        name="outproj",
    )(*args)


def _ffn_kernel(h_ref, wgu_ref, wd_ref, x_ref, gpost_ref, mod_ref, o_ref, *, tf):
    h = h_ref[0]
    dff = wd_ref.shape[0]
    acc = None
    for c in range(dff // tf):
        g = jnp.dot(h, wgu_ref[:, c * tf:(c + 1) * tf], preferred_element_type=F32)
        u = jnp.dot(h, wgu_ref[:, dff + c * tf:dff + (c + 1) * tf], preferred_element_type=F32)
        a = (_silu(g) * u).astype(BF16)
        part = jnp.dot(a, wd_ref[c * tf:(c + 1) * tf, :], preferred_element_type=F32)
        acc = part if acc is None else acc + part
    o_ref[0] = x_ref[0] + mod_ref[0, G2:G2 + 1, :] * _rms_rows(acc, gpost_ref[...])


def _ffn(h, w_gu, w_down, x, gpost, mod, mod_row, tm, tf):
    b, t, d = x.shape
    dff = w_down.shape[0]
    return pl.pallas_call(
        functools.partial(_ffn_kernel, tf=tf),
        grid=(b, t // tm),
        in_specs=[pl.BlockSpec((1, tm, d), lambda bb, i: (bb, i, 0)),
                  pl.BlockSpec((d, 2 * dff), lambda bb, i: (0, 0), pipeline_mode=pl.Buffered(1)),
                  pl.BlockSpec((dff, d), lambda bb, i: (0, 0), pipeline_mode=pl.Buffered(1)),
                  pl.BlockSpec((1, tm, d), lambda bb, i: (bb, i, 0)),
                  pl.BlockSpec((1, d), lambda bb, i: (0, 0)),
                  pl.BlockSpec((1, 6, d), lambda bb, i: (mod_row(bb), 0, 0))],
        out_specs=pl.BlockSpec((1, tm, d), lambda bb, i: (bb, i, 0)),
        out_shape=jax.ShapeDtypeStruct((b, t, d), F32),
        compiler_params=_cparams(("arbitrary", "arbitrary")),
        name="ffn",
    )(h, w_gu, w_down, x, gpost.reshape(1, d), mod)


def _attn_kernel(q_ref, kl_ref, vl_ref, kc_ref, vc_ref, lam_ref, subln_ref, oe_ref, oo_ref, ks, vs, sa, sb, ma, mb,
                 *, lambda_init, nstep, tq):
    i = pl.program_id(2)
    t = kl_ref.shape[1]

    @pl.when(i == 0)
    def _():
        ks[0:t, :] = kl_ref[0]
        ks[t:, :] = kc_ref[0]
        vs[0:t, 0:LANES] = vl_ref[0]
        vs[t:, 0:LANES] = vc_ref[0]
        vs[:, LANES:] = jnp.ones((vs.shape[0], LANES), BF16)

    def scores(half, s_scr, m_scr):
        q = q_ref[0, half * tq:(half + 1) * tq, :]
        k = ks[...]
        lane = lax.broadcasted_iota(I32, q.shape, 1)
        zero = jnp.zeros_like(q)
        for h, qh in enumerate((jnp.where(lane < DIFF_HEAD_DIM, q, zero),
                                jnp.where(lane >= DIFF_HEAD_DIM, q, zero))):
            s = lax.dot_general(qh, k, (((1,), (1,)), ((), ())), preferred_element_type=F32)
            s_scr[h] = s
            m_scr[h] = jnp.max(s, axis=-1, keepdims=True)

    def finish(s_scr, m_scr, o_ref):
        v = vs[...]
        lp = lam_ref[...]
        lam = (jnp.exp(jnp.sum(lp[0:1] * lp[1:2], axis=-1, keepdims=True))
               - jnp.exp(jnp.sum(lp[2:3] * lp[3:4], axis=-1, keepdims=True)) + lambda_init)
        outs = []
        for h in range(2):
            p = jnp.exp2(s_scr[h] - m_scr[h]).astype(BF16)
            r = jnp.dot(p, v, preferred_element_type=F32)
            outs.append(r[:, 0:LANES] / r[:, LANES:LANES + 1])
        o = outs[0] - lam * outs[1]
        o_ref[0] = (_rms_rows(o, subln_ref[...]) * (1.0 - lambda_init)).astype(o_ref.dtype)

    @pl.when(i == 0)
    def _():
        scores(0, sa, ma)

    @pl.when(jnp.logical_and(i > 0, i < nstep))
    def _():
        scores(0, sa, ma)
        finish(sb, mb, oo_ref)

    @pl.when(i < nstep)
    def _():
        scores(1, sb, mb)
        finish(sa, ma, oe_ref)

    @pl.when(i == nstep)
    def _():
        finish(sb, mb, oo_ref)


def _attention(z, zc, lam_params, subln, tq, lambda_init):
    b, t, n3 = z.shape
    tc = zc.shape[1]
    d = n3 // 3
    npair = d // LANES
    nstep = t // (2 * tq)
    s_scr = pltpu.VMEM((2, tq, t + tc), F32)
    m_scr = pltpu.VMEM((2, tq, 1), F32)
    half = jax.ShapeDtypeStruct((b, t // 2, d), BF16)
    return pl.pallas_call(
        functools.partial(_attn_kernel, lambda_init=lambda_init, nstep=nstep, tq=tq),
        grid=(b, npair, nstep + 1),
        in_specs=[pl.BlockSpec((1, 2 * tq, LANES), lambda bb, p, i: (bb, jnp.minimum(i, nstep - 1), p)),
                  pl.BlockSpec((1, t, LANES), lambda bb, p, i: (bb, 0, npair + p)),
                  pl.BlockSpec((1, t, LANES), lambda bb, p, i: (bb, 0, 2 * npair + p)),
                  pl.BlockSpec((1, tc, LANES), lambda bb, p, i: (bb, 0, npair + p)),
                  pl.BlockSpec((1, tc, LANES), lambda bb, p, i: (bb, 0, 2 * npair + p)),
                  pl.BlockSpec(lam_params.shape, lambda bb, p, i: (0, 0)),
                  pl.BlockSpec((1, LANES), lambda bb, p, i: (0, 0))],
        out_specs=[pl.BlockSpec((1, tq, LANES), lambda bb, p, i: (bb, jnp.minimum(i, nstep - 1), p)),
                   pl.BlockSpec((1, tq, LANES), lambda bb, p, i: (bb, jnp.maximum(i - 1, 0), p))],
        out_shape=[half, half],
        scratch_shapes=[pltpu.VMEM((t + tc, LANES), BF16), pltpu.VMEM((t + tc, 2 * LANES), BF16),
                        s_scr, s_scr, m_scr, m_scr],
        compiler_params=_cparams(("arbitrary", "arbitrary", "arbitrary")),
        name="attention",
    )(z, z, z, zc, zc, lam_params.astype(F32), subln.reshape(1, LANES).astype(F32))


def _moe_kernel(te_ref, tv_ref, pdst_ref, src_ref, nsrc_ref, wg_ref, wu_ref, wd_ref, h_hbm,
                y_hbm, gbuf, hbuf, acc, obuf, gsem, ssem, *, tm):
    del te_ref
    j = pl.program_id(0)
    f = pl.program_id(1)
    nf = pl.num_programs(1)
    slot = j % 2
    other = 1 - slot
    rows_per_step = tm // MOE_HIDDEN_STEPS
    valid = tv_ref[j] > 0
    drain = jnp.logical_and(jnp.logical_not(valid), tv_ref[jnp.maximum(j - 1, 0)] * (j > 0).astype(I32) > 0)

    def gather_row(idx_ref, to_slot, g, k):
        tok = idx_ref[0, 0, g * SUBLANES + k]
        pltpu.make_async_copy(h_hbm.at[tok], gbuf.at[to_slot, :, g, pl.ds(k, 1), :], gsem.at[to_slot]).start()

    def scatter_row(idx_ref, from_slot, g, k):
        row = idx_ref[0, 0, g * SUBLANES + k]
        pltpu.make_async_copy(obuf.at[from_slot, :, g, pl.ds(k, 1), :], y_hbm.at[:, pl.ds(row, 1), :],
                              ssem.at[from_slot]).start(priority=k % 2)

    def loop_rows(fn):
        def body(g, c):
            for k in range(SUBLANES):
                fn(g, k)
            return c
        lax.fori_loop(0, tm // SUBLANES, body, 0)

    def wait_gather(s):
        pltpu.make_async_copy(gbuf.at[1 - s], gbuf.at[s], gsem.at[s]).wait()

    def wait_scatter(s):
        pltpu.make_async_copy(obuf.at[1 - s], obuf.at[s], ssem.at[s]).wait()

    @pl.when(jnp.logical_and(j == 0, f == 0))
    def _():
        obuf[1] = jnp.zeros(obuf.shape[1:], U32)
        loop_rows(lambda g, k: gather_row(src_ref, 0, g, k))

    @pl.when(jnp.logical_and(jnp.logical_or(valid, drain), f == 0))
    def _():
        wait_gather(slot)

    @pl.when(jnp.logical_and(valid, f == 0))
    def _():
        n_slab = gbuf.shape[1]
        for sl in range(n_slab):
            lo, hi = _unpack_rows(gbuf[slot, sl].reshape(tm, LANES))
            hbuf[:, sl * LANES:(sl + 1) * LANES] = lo.astype(BF16)
            hbuf[:, (n_slab + sl) * LANES:(n_slab + sl + 1) * LANES] = hi.astype(BF16)
        acc[...] = jnp.zeros(acc.shape, F32)

    @pl.when(valid)
    def _():
        base = f * (rows_per_step // SUBLANES)
        for k in range(rows_per_step):
            gather_row(nsrc_ref, other, base + k // SUBLANES, k % SUBLANES)
            scatter_row(pdst_ref, other, base + k // SUBLANES, k % SUBLANES)
        h = hbuf[...]
        g = jnp.dot(h, wg_ref[0], preferred_element_type=F32)
        u = jnp.dot(h, wu_ref[0], preferred_element_type=F32)
        a = (_silu(g) * u).astype(BF16)
        acc[...] += jnp.dot(a, wd_ref[0], preferred_element_type=F32)

    @pl.when(jnp.logical_and(valid, f == nf - 1))
    def _():
        @pl.when(j > 0)
        def _():
            wait_scatter(slot)

        packed = _pack_rows(acc[...])
        for sl in range(obuf.shape[1]):
            obuf[slot, sl] = packed[:, sl * LANES:(sl + 1) * LANES].reshape(tm // SUBLANES, SUBLANES, LANES)

    @pl.when(jnp.logical_and(drain, f == 0))
    def _():
        loop_rows(lambda g, k: scatter_row(pdst_ref, other, g, k))
        wait_scatter(other)
        wait_scatter(slot)


def _moe(hp, te, tv, src, dst, w_gu, w_down, tm):
    nt = te.shape[0]
    n_tok, n_slab = hp.shape[:2]
    e, d, two_ff = w_gu.shape
    dff = two_ff // 2
    nf = MOE_HIDDEN_STEPS
    tf = dff // nf

    def wf(j, f, te_r, tv_r):
        return jnp.where(tv_r[j] > 0, f, nf - 1)

    def inv_spec(off):
        return pl.BlockSpec((1, 1, tm), lambda j, f, a, c: (jnp.minimum(j + off, nt), 0, 0),
                            memory_space=pltpu.SMEM)

    grid_spec = pltpu.PrefetchScalarGridSpec(
        num_scalar_prefetch=2,
        grid=(nt, nf),
        in_specs=[inv_spec(0), inv_spec(1), inv_spec(2),
                  pl.BlockSpec((1, d, tf), lambda j, f, a, c: (a[j], 0, wf(j, f, a, c))),
                  pl.BlockSpec((1, d, tf), lambda j, f, a, c: (a[j], 0, wf(j, f, a, c) + nf)),
                  pl.BlockSpec((1, tf, d), lambda j, f, a, c: (a[j], wf(j, f, a, c), 0)),
                  pl.BlockSpec(memory_space=pl.ANY)],
        out_specs=pl.BlockSpec(memory_space=pl.ANY),
        scratch_shapes=[pltpu.VMEM((2, n_slab, tm // SUBLANES, SUBLANES, LANES), U32),
                        pltpu.VMEM((tm, d), BF16),
                        pltpu.VMEM((tm, d), F32),
                        pltpu.VMEM((2, n_slab, tm // SUBLANES, SUBLANES, LANES), U32),
                        pltpu.SemaphoreType.DMA((2,)),
                        pltpu.SemaphoreType.DMA((2,))],
    )
    return pl.pallas_call(
        functools.partial(_moe_kernel, tm=tm),
        grid_spec=grid_spec,
        out_shape=jax.ShapeDtypeStruct((n_slab, 2 * n_tok + tm, LANES), U32),
        compiler_params=_cparams(("arbitrary", "arbitrary")),
        name="moe",
    )(te, tv, dst, src, src, w_gu, w_gu, w_down, hp)


def _route_plan(rt, tm, nt):
    t = rt.shape[0]
    e_flat = rt[:, 0:2].astype(I32).reshape(-1)
    bits = (2 * t - 1).bit_length()
    order = jnp.sort((e_flat << bits) | jnp.arange(2 * t, dtype=I32)) & ((1 << bits) - 1)
    counts = jnp.sum((e_flat[:, None] == jnp.arange(N_EXPERTS, dtype=I32)[None, :]).astype(I32), axis=0)
    tiles_e = (counts + tm - 1) // tm
    tile_end = jnp.cumsum(tiles_e)
    n_used = tile_end[-1]
    tid = jnp.arange(nt, dtype=I32)
    tv = (tid < n_used).astype(I32)
    te = jnp.sum((jnp.minimum(tid, n_used - 1)[:, None] >= tile_end[None, :]).astype(I32), axis=1)
    te = jnp.minimum(te, N_EXPERTS - 1)
    first_tile = (tile_end - tiles_e)[te]
    offset = ((tid - first_tile) * tm)[:, None] + jnp.arange(tm, dtype=I32)[None, :]
    live = jnp.logical_and(tv[:, None] > 0, offset < counts[te][:, None])
    entry = jnp.clip((jnp.cumsum(counts) - counts)[te][:, None] + offset, 0, 2 * t - 1)
    inv = jnp.where(live, order[entry.reshape(-1)].reshape(nt, tm), 2 * t).reshape(-1)
    inv = jnp.concatenate([jnp.full((tm,), 2 * t, I32), inv])
    pad_row = 2 * t + jnp.arange(inv.shape[0], dtype=I32) % tm
    src = jnp.minimum(inv >> 1, t - 1)
    dst = jnp.where(inv < 2 * t, (inv & 1) * t + (inv >> 1), pad_row)
    return te, tv, src.reshape(nt + 1, 1, tm), dst.reshape(nt + 1, 1, tm)


def _combine_kernel(y0_ref, y1_ref, rt_ref, x_ref, gpost_ref, mod_ref, o_ref):
    w0 = rt_ref[0, :, 2:3]
    w1 = rt_ref[0, :, 3:4]
    los, his = [], []
    for sl in range(y0_ref.shape[0]):
        lo0, hi0 = _unpack_rows(y0_ref[sl])
        lo1, hi1 = _unpack_rows(y1_ref[sl])
        los.append(w0 * lo0 + w1 * lo1)
        his.append(w0 * hi0 + w1 * hi1)
    y = jnp.concatenate(los + his, axis=1)
    o_ref[0] = x_ref[0] + mod_ref[0, G2:G2 + 1, :] * _rms_rows(y, gpost_ref[...])


def _combine(y2, rt, x, gpost, mod, tm):
    b, t, d = x.shape
    per_b = t // tm
    nblk = b * per_b
    return pl.pallas_call(
        _combine_kernel,
        grid=(b, per_b),
        in_specs=[pl.BlockSpec((d // 2 // LANES, tm, LANES), lambda bb, i: (0, bb * per_b + i, 0)),
                  pl.BlockSpec((d // 2 // LANES, tm, LANES), lambda bb, i: (0, nblk + bb * per_b + i, 0)),
                  pl.BlockSpec((1, tm, LANES), lambda bb, i: (bb, i, 0)),
                  pl.BlockSpec((1, tm, d), lambda bb, i: (bb, i, 0)),
                  pl.BlockSpec((1, d), lambda bb, i: (0, 0)),
                  pl.BlockSpec((1, 6, d), lambda bb, i: (bb, 0, 0))],
        out_specs=pl.BlockSpec((1, tm, d), lambda bb, i: (bb, i, 0)),
        out_shape=jax.ShapeDtypeStruct((b, t, d), F32),
        compiler_params=_cparams(("arbitrary", "arbitrary")),
        name="combine",
    )(y2, y2, rt, x, gpost.reshape(1, d), mod)


def _rope_angles(rows, head_dim):
    pos_r = jnp.repeat(jnp.arange(rows, dtype=F32), GRID_W)
    pos_c = jnp.tile(jnp.arange(GRID_W, dtype=F32), rows)
    axis_dim = head_dim // 2
    freqs = ROPE_BASE ** (-jnp.arange(0, axis_dim, 2, dtype=F32) / axis_dim)
    return pos_r[:, None] * freqs[None, :], pos_c[:, None] * freqs[None, :]


def _rope_tables(rows, head_dim, reps):
    ar, ac = _rope_angles(rows, head_dim)
    cos = jnp.concatenate([jnp.cos(ar), jnp.cos(ar), jnp.cos(ac), jnp.cos(ac)], axis=1)
    zr = jnp.zeros_like(ar)
    s_up = jnp.concatenate([-jnp.sin(ar), zr, -jnp.sin(ac), zr], axis=1)
    s_dn = jnp.concatenate([zr, jnp.sin(ar), zr, jnp.sin(ac)], axis=1)
    return jnp.stack([jnp.tile(cos, (1, reps)), jnp.tile(s_up, (1, reps)), jnp.tile(s_dn, (1, reps))])


def _identity_tables(n_tab, tm, period):
    return jnp.concatenate([jnp.ones((1, tm, period), F32), jnp.zeros((n_tab - 1, tm, period), F32)])


def _tile(t, want):
    return min(t, want)


def kernel(x, c, ctx, c_ctx, l0_w_mod, l0_b_mod, l0_norm_pre_mix, l0_norm_post_mix, l0_norm_pre_ffn, l0_norm_post_ffn, l0_ret_w_in, l0_ret_decay_logit, l0_ret_w_out, l0_ffn_w_gate_up, l0_ffn_w_down, l1_w_mod, l1_b_mod, l1_norm_pre_mix, l1_norm_post_mix, l1_norm_pre_ffn, l1_norm_post_ffn, l1_attn_w_in, l1_attn_lambda, l1_attn_subln, l1_attn_w_out, l1_router_w, l1_router_b, l1_moe_w_gate_up, l1_moe_w_down):
    b, t, d = x.shape
    tc = ctx.shape[1]
    rows = t // GRID_W
    assert t % tc == 0

    n_rows = -(-(b + 1) // 8) * 8
    cc = jnp.concatenate([c, c_ctx[None, :], jnp.zeros((n_rows - b - 1, d), F32)], axis=0)
    lat_row = lambda bb: bb
    ctx_row = lambda bb: b

    tm = 2 * OUTPROJ_ROW_GROUP
    tn = 1024
    assert t % tm == 0

    mod0 = _modulation(cc, l0_w_mod, l0_b_mod)
    w_in0 = l0_ret_w_in.astype(BF16)
    dk = w_in0.shape[1] // (8 * RET_HEADS)
    tab_ret = _rope_tables(rows, dk, 1)
    tab_ret = jnp.stack([tab_ret[0], tab_ret[1] + tab_ret[2]])
    ret_scales = (1.0,) * (RET_HEADS * dk // tn) + (dk ** -0.5,) * (RET_HEADS * dk // tn)
    z0 = _inproj(x, l0_norm_pre_mix, mod0, lat_row, w_in0, tab_ret, True, tm, tn, ret_scales, (dk // 4,))
    zc0 = _inproj(ctx, l0_norm_pre_mix, mod0, ctx_row, w_in0, _identity_tables(2, tc, dk), False, tc, tn,
                  ret_scales, (dk // 4,))
    yf, yb, yfc, ybc = _retention(z0, zc0, l0_ret_decay_logit)

    w_out0 = l0_ret_w_out.astype(BF16)
    x1, h1 = _outproj([yf, yb], w_out0, x, l0_norm_post_mix, l0_norm_pre_ffn, mod0, lat_row, tm)
    c1, hc1 = _outproj([yfc, ybc], w_out0, ctx, l0_norm_post_mix, l0_norm_pre_ffn, mod0, ctx_row, tc)

    w_gu0 = l0_ffn_w_gate_up.astype(BF16)
    w_dn0 = l0_ffn_w_down.astype(BF16)
    x2 = _ffn(h1, w_gu0, w_dn0, x1, l0_norm_post_ffn, mod0, lat_row, tm, FFN_HIDDEN_TILE)
    c2 = _ffn(hc1, w_gu0, w_dn0, c1, l0_norm_post_ffn, mod0, ctx_row, tc, FFN_HIDDEN_TILE)

    lambda_init = 0.8 - 0.6 * math.exp(-0.3 * 1)
    mod1 = _modulation(cc, l1_w_mod, l1_b_mod)
    w_in1 = l1_attn_w_in.astype(BF16)
    tab_diff = _rope_tables(rows, DIFF_HEAD_DIM, LANES // DIFF_HEAD_DIM)
    q4 = DIFF_HEAD_DIM // 4
    diff_scales = (DIFF_HEAD_DIM ** -0.5 * LOG2_E,) * (d // tn) + (1.0,) * (d // tn)
    z1 = _inproj(x2, l1_norm_pre_mix, mod1, lat_row, w_in1, tab_diff, True, tm, tn, diff_scales,
                 (LANES - q4, q4))
    zc1 = _inproj(c2, l1_norm_pre_mix, mod1, ctx_row, w_in1, _identity_tables(3, tc, LANES), False, tc, tn,
                  (1.0,) * (2 * d // tn), (LANES - q4, q4))
    o_even, o_odd = _attention(z1, zc1, l1_attn_lambda, l1_attn_subln, OUTPROJ_ROW_GROUP, lambda_init)

    rw = jnp.zeros((d, LANES), F32).at[:, :N_EXPERTS].set(l1_router_w)
    rb = jnp.full((1, LANES), NEG_BIG, F32).at[0, :N_EXPERTS].set(l1_router_b)
    x3, hp, rt = _outproj([o_even, o_odd], l1_attn_w_out.astype(BF16), x2, l1_norm_post_mix, l1_norm_pre_ffn,
                          mod1, lat_row, tm, router=(rw, rb), y_per_group=True)

    n_tok = b * t
    tm_e = _tile(n_tok, 1024)
    nt = 2 * n_tok // tm_e + N_EXPERTS
    te, tv, src, dst = _route_plan(rt.reshape(n_tok, LANES), tm_e, nt)
    y2 = _moe(hp.reshape(n_tok, d // 2 // LANES, 1, LANES), te, tv, src, dst, l1_moe_w_gate_up.astype(BF16),
              l1_moe_w_down.astype(BF16), tm_e)
    return _combine(y2, rt, x3, l1_norm_post_ffn, mod1, tm)
```
